```python
import math
import jax, jax.numpy as jnp
from jax import lax
import numpy as np

D_MODEL = 1024
BATCH = 16
SEQ = 4096
DEPTH = 1

PLE_DIM = 256
EPS = 1e-6
GN_EPS = 1e-5
NEG = -1e30

RET_HEADS = 8
RET_DV = D_MODEL // RET_HEADS
RET_DK = RET_DV // 2
RET_CHUNK = 128

NSA_HEADS = 16
NSA_GROUPS = 2
NSA_HPG = NSA_HEADS // NSA_GROUPS
NSA_DV = D_MODEL // NSA_HEADS
NSA_DK = 3 * NSA_DV // 2
CMP_LEN = 32
CMP_STRIDE = 16
SEL_LEN = 64
SEL_TOPN = 16
WINDOW = 512
NSA_Q_BLOCK = 32

D_FF = -(-8 * D_MODEL // (3 * 256)) * 256

IN_SPLITS = (
    RET_HEADS * RET_DK,
    RET_HEADS * RET_DK,
    RET_HEADS * RET_DV,
    RET_HEADS * RET_DV,
    NSA_HEADS * NSA_DK,
    NSA_GROUPS * NSA_DK,
    NSA_GROUPS * NSA_DV,
    NSA_GROUPS * NSA_DK,
    NSA_GROUPS * NSA_DV,
    NSA_GROUPS * NSA_DK,
    NSA_GROUPS * NSA_DV,
    3 * NSA_HEADS,
    2 * D_MODEL,
)
IN_TOTAL = sum(IN_SPLITS)

kernel_name = "hybrid_retention_nsa_block"


def rms(x):
    xf = x.astype(jnp.float32)
    return (xf * lax.rsqrt(jnp.mean(xf * xf, axis=-1, keepdims=True) + EPS)).astype(x.dtype)


def rms_norm(x, g):
    xf = x.astype(jnp.float32)
    y = xf * lax.rsqrt(jnp.mean(xf * xf, axis=-1, keepdims=True) + EPS)
    return (y * g.astype(jnp.float32)).astype(x.dtype)


def retention(q, k, v):
    B_, S, H, dk = q.shape
    dv = v.shape[-1]
    n_chunks = S // RET_CHUNK
    qf = q.astype(jnp.float32)
    kf = k.astype(jnp.float32) * (dk ** -0.5)
    vf = v.astype(jnp.float32)

    def to_chunks(t):
        return t.reshape(B_, n_chunks, RET_CHUNK, H, t.shape[-1]).transpose(1, 0, 3, 2, 4)

    qc, kc, vc = to_chunks(qf), to_chunks(kf), to_chunks(vf)
    log_g = jnp.log1p(-(2.0 ** (-5.0 - jnp.arange(H, dtype=jnp.float32))))
    idx = jnp.arange(RET_CHUNK, dtype=jnp.float32)
    diff = idx[:, None] - idx[None, :]
    dmat = jnp.exp(log_g[:, None, None] * jnp.maximum(diff, 0.0)) * (diff >= 0)
    xi = jnp.exp(log_g[:, None] * (idx + 1.0))
    zeta = jnp.exp(log_g[:, None] * (RET_CHUNK - 1.0 - idx))
    chunk_decay = jnp.exp(log_g * RET_CHUNK)

    def step(R, inp):
        qi, ki, vi = inp
        inner = jnp.einsum('bhcd,bhsd->bhcs', qi, ki) * dmat[None]
        o = (jnp.einsum('bhcs,bhse->bhce', inner, vi)
             + jnp.einsum('bhcd,bhde->bhce', qi, R) * xi[None, :, :, None])
        R = (R * chunk_decay[None, :, None, None]
             + jnp.einsum('bhsd,bhse->bhde', ki * zeta[None, :, :, None], vi))
        return R, o

    R0 = jnp.zeros((B_, H, dk, dv), jnp.float32)
    _, o = lax.scan(step, R0, (qc, kc, vc))
    return o.transpose(1, 0, 3, 2, 4).reshape(B_, S, H, dv)


def group_norm_heads(o, g):
    mu = jnp.mean(o, axis=-1, keepdims=True)
    var = jnp.mean(jnp.square(o - mu), axis=-1, keepdims=True)
    y = (o - mu) * lax.rsqrt(var + GN_EPS)
    B_, S, H, dv = o.shape
    return y.reshape(B_, S, H * dv) * g.astype(jnp.float32)


def compress_blocks(src, pe, w1, w2):
    B_, S, G, d = src.shape
    n_cmp = (S - CMP_LEN) // CMP_STRIDE + 1
    idx = jnp.arange(n_cmp)[:, None] * CMP_STRIDE + jnp.arange(CMP_LEN)[None, :]
    blocks = src[:, idx] + pe[:, None, :]
    blocks = blocks.transpose(0, 3, 1, 2, 4).reshape(B_, G, n_cmp, CMP_LEN * d)
    return jax.nn.gelu(blocks @ w1) @ w2


def masked_softmax(s, mask):
    s = jnp.where(mask, s.astype(jnp.float32), NEG)
    return jax.nn.softmax(s, axis=-1) * mask


def gather_blocks(blocks, idx):
    return blocks[idx]


def nsa_attention(q, k_c, v_c, k_s, v_s, k_w, v_w, gates,
                  pe_k, w1_k, w2_k, pe_v, w1_v, w2_v):
    B_, S, H, dk = q.shape
    G, hpg, dv = NSA_GROUPS, NSA_HPG, NSA_DV
    scale = dk ** -0.5
    slopes = (2.0 ** (-8.0 * (jnp.arange(H, dtype=jnp.float32) + 1.0) / H)).reshape(G, hpg)
    sl = slopes[None, :, :, None, None]

    k_cmp = compress_blocks(k_c, pe_k, w1_k, w2_k)
    v_cmp = compress_blocks(v_c, pe_v, w1_v, w2_v)
    n_cmp = k_cmp.shape[2]
    cmp_start = jnp.arange(n_cmp) * CMP_STRIDE
    cmp_end = cmp_start + CMP_LEN - 1

    n_sel = S // SEL_LEN
    topn = min(SEL_TOPN, n_sel)
    sel_start = jnp.arange(n_sel) * SEL_LEN
    overlap = ((cmp_start[:, None] < sel_start[None, :] + SEL_LEN)
               & (cmp_start[:, None] + CMP_LEN > sel_start[None, :])).astype(jnp.float32)
    ks_blocks = k_s.reshape(B_, n_sel, SEL_LEN, G, dk).transpose(0, 3, 1, 2, 4)
    vs_blocks = v_s.reshape(B_, n_sel, SEL_LEN, G, dv).transpose(0, 3, 1, 2, 4)

    kw_pad = jnp.pad(k_w.transpose(0, 2, 1, 3), ((0, 0), (0, 0), (WINDOW, 0), (0, 0)))
    vw_pad = jnp.pad(v_w.transpose(0, 2, 1, 3), ((0, 0), (0, 0), (WINDOW, 0), (0, 0)))

    n_q = S // NSA_Q_BLOCK
    q_chunks = q.reshape(B_, n_q, NSA_Q_BLOCK, G, hpg, dk).transpose(1, 0, 3, 4, 2, 5)
    g_chunks = gates.reshape(B_, n_q, NSA_Q_BLOCK, 3, G, hpg).transpose(1, 3, 0, 4, 5, 2)
    c_ids = jnp.arange(n_q, dtype=jnp.int32)

    def one_block(inp):
        c, qg, gg = inp
        t = c * NSA_Q_BLOCK + jnp.arange(NSA_Q_BLOCK)
        tf = t.astype(jnp.float32)

        s_c = jnp.einsum('bghtd,bgnd->bghtn', qg, k_cmp) * scale
        s_c = s_c - sl * (tf[:, None] - cmp_end[None, :].astype(jnp.float32))
        valid_c = cmp_end[None, :] <= t[:, None]
        p_c = masked_softmax(s_c, valid_c)
        o_c = jnp.einsum('bghtn,bgne->bghte', p_c, v_cmp.astype(jnp.float32))

        imp = jnp.einsum('bgtn,nj->bgtj', jnp.sum(p_c, axis=2), overlap)
        blk = jnp.arange(n_sel)
        cur = t // SEL_LEN
        valid_s = sel_start[None, :] <= t[:, None]
        forced = (blk[None, :] == 0) | (blk[None, :] == cur[:, None]) | (blk[None, :] == cur[:, None] - 1)
        score = jnp.where(forced, 1e6, jnp.where(valid_s, imp, -1e6))
        _, sel = lax.top_k(score, topn)
        k_g = jax.vmap(jax.vmap(gather_blocks))(ks_blocks, sel).reshape(B_, G, NSA_Q_BLOCK, topn * SEL_LEN, dk)
        v_g = jax.vmap(jax.vmap(gather_blocks))(vs_blocks, sel).reshape(B_, G, NSA_Q_BLOCK, topn * SEL_LEN, dv)
        pos_s = (sel[..., None] * SEL_LEN + jnp.arange(SEL_LEN)).reshape(B_, G, NSA_Q_BLOCK, topn * SEL_LEN)
        s_s = jnp.einsum('bghtd,bgtkd->bghtk', qg, k_g) * scale
        s_s = s_s - sl * (tf[None, None, None, :, None] - pos_s[:, :, None].astype(jnp.float32))
        valid_sel = (pos_s <= t[None, None, :, None])[:, :, None]
        p_s = masked_softmax(s_s, valid_sel)
        o_s = jnp.einsum('bghtk,bgtke->bghte', p_s, v_g.astype(jnp.float32))

        start = c * NSA_Q_BLOCK
        k_win = lax.dynamic_slice_in_dim(kw_pad, start, NSA_Q_BLOCK + WINDOW, axis=2)
        v_win = lax.dynamic_slice_in_dim(vw_pad, start, NSA_Q_BLOCK + WINDOW, axis=2)
        pos_w = start - WINDOW + jnp.arange(NSA_Q_BLOCK + WINDOW)
        s_w = jnp.einsum('bghtd,bgkd->bghtk', qg, k_win) * scale
        s_w = s_w - sl * (tf[:, None] - pos_w[None, :].astype(jnp.float32))
        valid_w = (pos_w[None, :] <= t[:, None]) & (pos_w[None, :] > t[:, None] - WINDOW) & (pos_w[None, :] >= 0)
        p_w = masked_softmax(s_w, valid_w)
        o_w = jnp.einsum('bghtk,bgke->bghte', p_w, v_win.astype(jnp.float32))

        gf = gg.astype(jnp.float32)[..., None]
        return gf[0] * o_c + gf[1] * o_s + gf[2] * o_w

    out = lax.map(one_block, (c_ids, q_chunks, g_chunks))
    out = out.transpose(1, 0, 4, 2, 3, 5).reshape(B_, S, H * dv)
    return out.astype(q.dtype)


def setup_inputs(seed: int = 0) -> dict:
    key = jax.random.key(seed)
    ks = jax.random.split(key, 20)
    f32 = jnp.float32

    def w(k, shape, fan_in):
        return jax.random.normal(k, shape, f32) * (fan_in ** -0.5)

    def gain(k, shape):
        return 1.0 + 0.05 * jax.random.normal(k, shape, f32)

    return {
        "x": jax.random.normal(ks[0], (BATCH, SEQ, D_MODEL), f32),
        "p": jax.random.normal(ks[1], (DEPTH, BATCH, SEQ, PLE_DIM), f32),
        "norm_mix_pre": gain(ks[2], (DEPTH, D_MODEL)),
        "norm_mix_post": gain(ks[3], (DEPTH, D_MODEL)),
        "norm_ffn_pre": gain(ks[4], (DEPTH, D_MODEL)),
        "norm_ffn_post": gain(ks[5], (DEPTH, D_MODEL)),
        "w_in": w(ks[6], (DEPTH, D_MODEL, IN_TOTAL), D_MODEL),
        "ret_gn_g": gain(ks[7], (DEPTH, RET_HEADS * RET_DV)),
        "cmp_pe_k": 0.1 * jax.random.normal(ks[8], (DEPTH, CMP_LEN, NSA_DK), f32),
        "cmp_w1_k": w(ks[9], (DEPTH, CMP_LEN * NSA_DK, NSA_DK), CMP_LEN * NSA_DK),
        "cmp_w2_k": w(ks[10], (DEPTH, NSA_DK, NSA_DK), NSA_DK),
        "cmp_pe_v": 0.1 * jax.random.normal(ks[11], (DEPTH, CMP_LEN, NSA_DV), f32),
        "cmp_w1_v": w(ks[12], (DEPTH, CMP_LEN * NSA_DV, NSA_DV), CMP_LEN * NSA_DV),
        "cmp_w2_v": w(ks[13], (DEPTH, NSA_DV, NSA_DV), NSA_DV),
        "w_out": w(ks[14], (DEPTH, D_MODEL, D_MODEL), D_MODEL),
        "w_ffn_in": w(ks[15], (DEPTH, D_MODEL, 2 * D_FF), D_MODEL),
        "w_ffn_out": w(ks[16], (DEPTH, D_FF, D_MODEL), D_FF),
        "w_ple": w(ks[17], (DEPTH, PLE_DIM, D_MODEL), PLE_DIM),
        "w_ple_gate": w(ks[18], (DEPTH, D_MODEL, D_MODEL), D_MODEL),
    }


def reference(x, p, norm_mix_pre, norm_mix_post, norm_ffn_pre, norm_ffn_post, w_in, ret_gn_g,
              cmp_pe_k, cmp_w1_k, cmp_w2_k, cmp_pe_v, cmp_w1_v, cmp_w2_v,
              w_out, w_ffn_in, w_ffn_out, w_ple, w_ple_gate):
    B_, S, _ = x.shape
    offsets = [int(o) for o in np.cumsum(IN_SPLITS)[:-1]]
    for i in range(DEPTH):
        h = rms_norm(x, norm_mix_pre[i])
        proj = h @ w_in[i]
        (r_q, r_k, r_v, r_g, n_q, kc, vc, ks_, vs_, kw, vw, n_g, m_g) = jnp.split(proj, offsets, axis=-1)

        ret = retention(r_q.reshape(B_, S, RET_HEADS, RET_DK),
                        r_k.reshape(B_, S, RET_HEADS, RET_DK),
                        r_v.reshape(B_, S, RET_HEADS, RET_DV))
        ret = (jax.nn.silu(r_g.astype(jnp.float32)) * group_norm_heads(ret, ret_gn_g[i])).astype(x.dtype)

        nsa = nsa_attention(
            n_q.reshape(B_, S, NSA_HEADS, NSA_DK),
            kc.reshape(B_, S, NSA_GROUPS, NSA_DK), vc.reshape(B_, S, NSA_GROUPS, NSA_DV),
            ks_.reshape(B_, S, NSA_GROUPS, NSA_DK), vs_.reshape(B_, S, NSA_GROUPS, NSA_DV),
            kw.reshape(B_, S, NSA_GROUPS, NSA_DK), vw.reshape(B_, S, NSA_GROUPS, NSA_DV),
            jax.nn.sigmoid(n_g).reshape(B_, S, 3, NSA_HEADS),
            cmp_pe_k[i], cmp_w1_k[i], cmp_w2_k[i], cmp_pe_v[i], cmp_w1_v[i], cmp_w2_v[i])

        g_a, g_b = jnp.split(jax.nn.sigmoid(m_g), 2, axis=-1)
        mix = (g_a * ret + g_b * nsa) @ w_out[i]
        x = x + rms_norm(mix, norm_mix_post[i])

        h = rms_norm(x, norm_ffn_pre[i])
        u, v = jnp.split(h @ w_ffn_in[i], 2, axis=-1)
        f = (jax.nn.silu(u) * v) @ w_ffn_out[i]
        x = x + rms_norm(f, norm_ffn_post[i])

        gate = jax.nn.sigmoid(rms(x) @ w_ple_gate[i])
        x = x + gate * (p[i] @ w_ple[i])
    return x
```

```python
import functools

import jax
import jax.numpy as jnp
from jax import lax
from jax.experimental import pallas as pl
from jax.experimental.pallas import tpu as pltpu

F32 = jnp.float32
BF16 = jnp.bfloat16

D_MODEL = 1024
PLE_DIM = 256
EPS = 1e-6
GN_EPS = 1e-5
NEG = -1e30

RET_HEADS = 8
RET_DV = D_MODEL // RET_HEADS
RET_DK = RET_DV // 2
RET_CHUNK = 128

NSA_HEADS = 16
NSA_GROUPS = 2
NSA_HPG = NSA_HEADS // NSA_GROUPS
NSA_DV = D_MODEL // NSA_HEADS
NSA_DK = 3 * NSA_DV // 2
CMP_LEN = 32
CMP_STRIDE = 16
SEL_LEN = 64
SEL_TOPN = 16
WINDOW = 512
D_FF = -(-8 * D_MODEL // (3 * 256)) * 256

IN_SPLITS = (
    RET_HEADS * RET_DK, RET_HEADS * RET_DK, RET_HEADS * RET_DV, RET_HEADS * RET_DV,
    NSA_HEADS * NSA_DK, NSA_GROUPS * NSA_DK, NSA_GROUPS * NSA_DV, NSA_GROUPS * NSA_DK,
    NSA_GROUPS * NSA_DV, NSA_GROUPS * NSA_DK, NSA_GROUPS * NSA_DV, 3 * NSA_HEADS, 2 * D_MODEL,
)

LANES = 128
Q_TILE = 128
SEL_KEY_TILE = 256
WIN_KEYS = WINDOW + Q_TILE
ALIBI_COL = NSA_DK
ONES_COL = NSA_DV
VMEM_LIMIT = 56 * 1024 * 1024

PA_RQ, PA_RK, PA_RV = 0, 512, 1024
PA_NQ = 2048
PA_KS = PA_NQ + NSA_HEADS * LANES
PA_KW = PA_KS + NSA_GROUPS * LANES
PA_VS = PA_KW + NSA_GROUPS * LANES
PA_VW = PA_VS + NSA_GROUPS * LANES
PA_TOTAL = PA_VW + NSA_GROUPS * LANES
PF_KC, PF_VC = 0, NSA_GROUPS * NSA_DK
PF_NG = 384
PF_TOTAL = PF_NG + NSA_GROUPS * LANES


def _dot(a, b):
    return jnp.dot(a, b, preferred_element_type=F32)


def _dot_nt(a, b):
    return lax.dot_general(a, b, (((1,), (1,)), ((), ())), preferred_element_type=F32)


def _dot_tn(a, b):
    return lax.dot_general(a, b, (((0,), (0,)), ((), ())), preferred_element_type=F32)


def _rms(x):
    return x * lax.rsqrt(jnp.mean(x * x, axis=-1, keepdims=True) + EPS)


def _const_spec(shape):
    nd = len(shape)
    return pl.BlockSpec(shape, lambda *_: (0,) * nd, pipeline_mode=pl.Buffered(1))


def _proj_kernel(x_ref, g_ref, w_ref, o_ref, *, chunk):
    h = (_rms(x_ref[...]) * g_ref[...]).astype(BF16)
    n = o_ref.shape[-1]
    for c0 in range(0, n, chunk):
        c1 = min(c0 + chunk, n)
        o_ref[:, c0:c1] = _dot(h, w_ref[:, c0:c1]).astype(o_ref.dtype)


def _proj(x2, g, w, out_dtype, tm, chunk):
    t, d = x2.shape
    n = w.shape[1]
    return pl.pallas_call(
        functools.partial(_proj_kernel, chunk=chunk),
        grid=(t // tm,),
        in_specs=[pl.BlockSpec((tm, d), lambda i: (i, 0)), _const_spec((1, d)), _const_spec((d, n))],
        out_specs=pl.BlockSpec((tm, n), lambda i: (i, 0)),
        out_shape=jax.ShapeDtypeStruct((t, n), out_dtype),
        compiler_params=pltpu.CompilerParams(dimension_semantics=("parallel",), vmem_limit_bytes=VMEM_LIMIT),
        name="proj_" + jnp.dtype(out_dtype).name,
    )(x2, g, w)


def _ret_kernel(q_ref, k_ref, v_ref, dmat_ref, xi_ref, zeta_ref, dec_ref, o_ref, state_ref):
    @pl.when(pl.program_id(1) == 0)
    def _():
        state_ref[...] = jnp.zeros_like(state_ref)

    for h in range(RET_HEADS):
        q = q_ref[0, :, h * RET_DK:(h + 1) * RET_DK]
        k = k_ref[0, :, h * RET_DK:(h + 1) * RET_DK]
        v = v_ref[0, :, h * RET_DV:(h + 1) * RET_DV]
        r = state_ref[h]
        inner = _dot_nt(q, k) * dmat_ref[h]
        o = _dot(inner.astype(BF16), v) + _dot(q, r.astype(BF16)) * xi_ref[h]
        kz = (k.astype(F32) * zeta_ref[h]).astype(BF16)
        state_ref[h] = r * dec_ref[h] + _dot_tn(kz, v)
        mu = jnp.mean(o, axis=-1, keepdims=True)
        d = o - mu
        var = jnp.mean(d * d, axis=-1, keepdims=True)
        o_ref[0, :, h * RET_DV:(h + 1) * RET_DV] = (d * lax.rsqrt(var + GN_EPS)).astype(o_ref.dtype)


def _retention(pa3):
    b, s, _ = pa3.shape
    c = RET_CHUNK
    hh = RET_HEADS
    log_g = jnp.log1p(-(2.0 ** (-5.0 - jnp.arange(hh, dtype=F32))))
    idx = jnp.arange(c, dtype=F32)
    diff = idx[:, None] - idx[None, :]
    dmat = jnp.exp(log_g[:, None, None] * jnp.maximum(diff, 0.0)) * (diff >= 0)
    xi = jnp.broadcast_to(jnp.exp(log_g[:, None] * (idx + 1.0))[:, :, None], (hh, c, RET_DV))
    zeta = jnp.broadcast_to(jnp.exp(log_g[:, None] * (c - 1.0 - idx))[:, :, None], (hh, c, RET_DK))
    dec = jnp.broadcast_to(jnp.exp(log_g * c)[:, None, None], (hh, RET_DK, RET_DV))
    wq = hh * RET_DK
    wv = hh * RET_DV
    return pl.pallas_call(
        _ret_kernel,
        grid=(b, s // c),
        in_specs=[
            pl.BlockSpec((1, c, wq), lambda i, j: (i, j, PA_RQ // wq)),
            pl.BlockSpec((1, c, wq), lambda i, j: (i, j, PA_RK // wq)),
            pl.BlockSpec((1, c, wv), lambda i, j: (i, j, PA_RV // wv)),
            _const_spec((hh, c, c)), _const_spec((hh, c, RET_DV)), _const_spec((hh, c, RET_DK)),
            _const_spec((hh, RET_DK, RET_DV)),
        ],
        out_specs=pl.BlockSpec((1, c, wv), lambda i, j: (i, j, 0)),
        out_shape=jax.ShapeDtypeStruct((b, s, wv), BF16),
        scratch_shapes=[pltpu.VMEM((hh, RET_DK, RET_DV), F32)],
        compiler_params=pltpu.CompilerParams(dimension_semantics=("parallel", "arbitrary")),
        name="retention",
    )(pa3, pa3, pa3, dmat, xi, zeta, dec)


def _cmp_kernel(h_ref, pe_top_ref, pe_bot_ref, w1t_ref, w1b_ref, w2_ref, tab_ref, o_ref):
    hb = h_ref[0, 0]
    top = _dot((hb + pe_top_ref[...]).astype(BF16), w1t_ref[...])
    bot = _dot((hb + pe_bot_ref[...]).astype(BF16), w1b_ref[...])
    n = bot.shape[0]
    pre = top + pltpu.roll(bot, n - 1, 0)
    act = jax.nn.gelu(pre)
    o_ref[0, 0] = (_dot(act.astype(BF16), w2_ref[...]) + tab_ref[...]).astype(o_ref.dtype)


def _compress(hblk, pe, w1, w2, tab):
    b, g, n, wd = hblk.shape
    d = wd // CMP_STRIDE
    pe_top = pe[:CMP_STRIDE].reshape(1, wd)
    pe_bot = pe[CMP_STRIDE:].reshape(1, wd)
    w1t = w1[:wd].astype(BF16)
    w1b = w1[wd:].astype(BF16)
    w2p = jnp.pad(w2, ((0, 0), (0, LANES - w2.shape[1]))).astype(BF16)
    return pl.pallas_call(
        _cmp_kernel,
        grid=(b, g),
        in_specs=[pl.BlockSpec((1, 1, n, wd), lambda i, j: (i, j, 0, 0)),
                  _const_spec((1, wd)), _const_spec((1, wd)), _const_spec((wd, d)), _const_spec((wd, d)),
                  _const_spec((d, LANES)), _const_spec((n, LANES))],
        out_specs=pl.BlockSpec((1, 1, n, LANES), lambda i, j: (i, j, 0, 0)),
        out_shape=jax.ShapeDtypeStruct((b, g, n, LANES), BF16),
        compiler_params=pltpu.CompilerParams(dimension_semantics=("parallel", "parallel")),
        name="compress_%d" % d,
    )(hblk, pe_top, pe_bot, w1t, w1b, w2p, tab)


def _split3(x):
    p1 = x.astype(BF16)
    r1 = x - p1.astype(F32)
    p2 = r1.astype(BF16)
    p3 = (r1 - p2.astype(F32)).astype(BF16)
    return p1, p2, p3


def _nsa_kernel(q_ref, ng_ref, kc_ref, vc_ref, ks_ref, kw_ref, vs_ref, vw_ref,
                posk_ref, oh_ref, qal_ref, ovt_ref, ones_ref, o_ref, m_ref, acc_ref, *, topn):
    hpg = NSA_HPG
    tq = Q_TILE
    rows = hpg * tq
    t0 = pl.program_id(2) * tq
    n_sel = ovt_ref.shape[0]
    n_cmp = kc_ref.shape[2]

    q_all = q_ref[0]
    q128 = jnp.concatenate([q_all[:, h * LANES:(h + 1) * LANES] for h in range(hpg)], axis=0) + qal_ref[0]

    s_c = _dot_nt(q128, kc_ref[0, 0]).reshape(hpg, tq, n_cmp)
    t_c = t0 + lax.broadcasted_iota(jnp.int32, (tq, n_cmp), 0)
    end_c = lax.broadcasted_iota(jnp.int32, (tq, n_cmp), 1) * CMP_STRIDE + (CMP_LEN - 1)
    valid_c = (end_c <= t_c)[None]
    s_c = jnp.where(valid_c, s_c, NEG)
    e_c = jnp.exp(s_c - jnp.max(s_c, axis=-1, keepdims=True)) * valid_c.astype(F32)
    l_c = jnp.sum(e_c, axis=-1, keepdims=True)
    p_c = e_c / jnp.where(l_c > 0.0, l_c, 1.0)
    o_c = _dot(p_c.reshape(rows, n_cmp).astype(BF16), vc_ref[0, 0])

    psum = jnp.sum(p_c, axis=0)
    ovt = ovt_ref[...]
    imp_t = sum(_dot_nt(ovt, piece) for piece in _split3(psum))
    jj = lax.broadcasted_iota(jnp.int32, (n_sel, tq), 0)
    tt = t0 + lax.broadcasted_iota(jnp.int32, (n_sel, tq), 1)
    cur = tt // SEL_LEN
    forced = (jj == 0) | (jj == cur) | (jj == cur - 1)
    score = jnp.where(forced, 1e6, jnp.where(jj * SEL_LEN <= tt, imp_t, -1e6))
    rank = jnp.zeros((n_sel, tq), F32)
    for i in range(n_sel):
        si = score[i:i + 1, :]
        rank = rank + jnp.where(jj > i, jnp.where(si >= score, 1.0, 0.0), jnp.where(si > score, 1.0, 0.0))
    pen_t = jnp.where(rank < float(topn), 0.0, NEG)
    pen_t = jnp.concatenate([pen_t, jnp.zeros((LANES - n_sel, tq), F32)], axis=0)
    pen = pen_t.T.astype(BF16)
    q256 = jnp.concatenate([q128, jnp.concatenate([pen] * hpg, axis=0)], axis=1)

    tk = SEL_KEY_TILE
    m_ref[...] = jnp.full(m_ref.shape, NEG, F32)
    acc_ref[...] = jnp.zeros(acc_ref.shape, F32)
    ones_v = ones_ref[0:tk, :]

    def sel_tile(kt, causal):
        k0 = pl.multiple_of(kt * tk, tk)
        k256 = jnp.concatenate([ks_ref[0, pl.ds(k0, tk), :] + posk_ref[pl.ds(k0, tk), :],
                                oh_ref[pl.ds(k0, tk), :]], axis=1)
        s = _dot_nt(q256, k256)
        if causal:
            pos = k0 + lax.broadcasted_iota(jnp.int32, (tq, tk), 1)
            tqi = t0 + lax.broadcasted_iota(jnp.int32, (tq, tk), 0)
            bias = jnp.where(pos <= tqi, 0.0, NEG)
            s = (s.reshape(hpg, tq, tk) + bias[None]).reshape(rows, tk)
        m_prev = m_ref[...]
        m_new = jnp.maximum(m_prev, jnp.max(s, axis=-1, keepdims=True))
        p = jnp.exp(s - jnp.concatenate([m_new] * (tk // LANES), axis=1))
        v = vs_ref[0, pl.ds(k0, tk), :] + ones_v
        acc_ref[...] = acc_ref[...] * jnp.exp(m_prev - m_new) + _dot(p.astype(BF16), v)
        m_ref[...] = m_new

    n_full = t0 // tk

    def body(kt, carry):
        sel_tile(kt, False)
        return carry

    lax.fori_loop(0, n_full, body, 0)
    sel_tile(n_full, True)
    acc = acc_ref[...]
    o_s = acc / acc[:, ONES_COL:ONES_COL + 1]

    w0 = pl.multiple_of(jnp.maximum(t0 - WINDOW, 0), LANES)
    kw = kw_ref[0, pl.ds(w0, WIN_KEYS), :] + posk_ref[pl.ds(w0, WIN_KEYS), :]
    s_w = _dot_nt(q128, kw).reshape(hpg, tq, WIN_KEYS)
    pos_w = w0 + lax.broadcasted_iota(jnp.int32, (tq, WIN_KEYS), 1)
    t_w = t0 + lax.broadcasted_iota(jnp.int32, (tq, WIN_KEYS), 0)
    valid_w = ((pos_w <= t_w) & (pos_w > t_w - WINDOW))[None]
    s_w = jnp.where(valid_w, s_w, NEG)
    e_w = jnp.exp(s_w - jnp.max(s_w, axis=-1, keepdims=True)) * valid_w.astype(F32)
    acc_w = _dot(e_w.reshape(rows, WIN_KEYS).astype(BF16), vw_ref[0, pl.ds(w0, WIN_KEYS), :] + ones_ref[...])
    o_w = acc_w / acc_w[:, ONES_COL:ONES_COL + 1]

    gate = jax.nn.sigmoid(ng_ref[0])
    outs = []
    for h in range(hpg):
        r0, r1 = h * tq, (h + 1) * tq
        outs.append(gate[:, h:h + 1] * o_c[r0:r1, :NSA_DV]
                    + gate[:, hpg + h:hpg + h + 1] * o_s[r0:r1, :NSA_DV]
                    + gate[:, 2 * hpg + h:2 * hpg + h + 1] * o_w[r0:r1, :NSA_DV])
    o_ref[0] = jnp.concatenate(outs, axis=1).astype(o_ref.dtype)


def _nsa(pa3, pf3, kcmp, vcmp):
    b, s, _ = pa3.shape
    g, hpg = NSA_GROUPS, NSA_HPG
    n_sel = s // SEL_LEN
    n_cmp = kcmp.shape[2]
    topn = min(SEL_TOPN, n_sel)
    rows = hpg * Q_TILE

    pos = jnp.arange(s, dtype=jnp.int32)
    ab = jnp.stack([pos // LANES] * 3 + [pos % LANES] * 3, axis=1).astype(F32)
    posk = jnp.zeros((s, LANES), F32).at[:, ALIBI_COL:ALIBI_COL + 6].set(ab).astype(BF16)
    onehot = (pos[:, None] // SEL_LEN == jnp.arange(LANES)[None, :]).astype(BF16)
    slopes = 2.0 ** (-8.0 * (jnp.arange(NSA_HEADS, dtype=F32) + 1.0) / NSA_HEADS)
    s1, s2, s3 = _split3(slopes)
    qcols = jnp.stack([s1.astype(F32) * LANES, s2.astype(F32) * LANES, s3.astype(F32) * LANES,
                       s1.astype(F32), s2.astype(F32), s3.astype(F32)], axis=1)
    qal = jnp.zeros((NSA_HEADS, LANES), F32).at[:, ALIBI_COL:ALIBI_COL + 6].set(qcols)
    qal = jnp.broadcast_to(qal.reshape(g, hpg, 1, LANES), (g, hpg, Q_TILE, LANES)).reshape(g, rows, LANES)
    qal = qal.astype(BF16)
    cstart = jnp.arange(n_cmp) * CMP_STRIDE
    sstart = jnp.arange(n_sel) * SEL_LEN
    ovt = ((cstart[None, :] < sstart[:, None] + SEL_LEN) & (cstart[None, :] + CMP_LEN > sstart[:, None])
           & (jnp.arange(n_cmp)[None, :] < n_cmp - 1)).astype(BF16)
    ones = jnp.zeros((WIN_KEYS, LANES), BF16).at[:, ONES_COL].set(1.0)

    def seq_spec(col0):
        return pl.BlockSpec((1, s, LANES), lambda i, j, t: (i, 0, col0 // LANES + j))

    return pl.pallas_call(
        functools.partial(_nsa_kernel, topn=topn),
        grid=(b, g, s // Q_TILE),
        in_specs=[
            pl.BlockSpec((1, Q_TILE, hpg * LANES), lambda i, j, t: (i, t, PA_NQ // (hpg * LANES) + j)),
            pl.BlockSpec((1, Q_TILE, LANES), lambda i, j, t: (i, t, PF_NG // LANES + j)),
            pl.BlockSpec((1, 1, n_cmp, LANES), lambda i, j, t: (i, j, 0, 0)),
            pl.BlockSpec((1, 1, n_cmp, LANES), lambda i, j, t: (i, j, 0, 0)),
            seq_spec(PA_KS), seq_spec(PA_KW), seq_spec(PA_VS), seq_spec(PA_VW),
            _const_spec((s, LANES)), _const_spec((s, LANES)),
            pl.BlockSpec((1, rows, LANES), lambda i, j, t: (j, 0, 0)),
            _const_spec((n_sel, n_cmp)), _const_spec((WIN_KEYS, LANES)),
        ],
        out_specs=pl.BlockSpec((1, Q_TILE, hpg * NSA_DV), lambda i, j, t: (i, t, j)),
        out_shape=jax.ShapeDtypeStruct((b, s, NSA_HEADS * NSA_DV), BF16),
        scratch_shapes=[pltpu.VMEM((rows, LANES), F32), pltpu.VMEM((rows, LANES), F32)],
        compiler_params=pltpu.CompilerParams(dimension_semantics=("parallel", "parallel", "parallel"),
                                             vmem_limit_bytes=VMEM_LIMIT),
        name="nsa_attention",
    )(pa3, pf3, kcmp, vcmp, pa3, pa3, pa3, pa3, posk, onehot, qal, ovt, ones)


def _post_kernel(x_ref, ret_ref, nsa_ref, p_ref, gpre_ref, gpost_ref, gfpre_ref, gfpost_ref, gn_ref,
                 wg_ref, wo_ref, wfi_ref, wfo_ref, wpg_ref, wpl_ref, o_ref, *, ff_chunk):
    d = D_MODEL
    x = x_ref[...]
    h = (_rms(x) * gpre_ref[...]).astype(BF16)
    ret = jax.nn.silu(_dot(h, wg_ref[:, 0:d])) * (ret_ref[...].astype(F32) * gn_ref[...])
    mixin = jax.nn.sigmoid(_dot(h, wg_ref[:, d:2 * d])) * ret
    mixin = mixin + jax.nn.sigmoid(_dot(h, wg_ref[:, 2 * d:3 * d])) * nsa_ref[...].astype(F32)
    x1 = x + _rms(_dot(mixin.astype(BF16), wo_ref[...])) * gpost_ref[...]

    h2 = (_rms(x1) * gfpre_ref[...]).astype(BF16)
    f = jnp.zeros(x.shape, F32)
    for c0 in range(0, D_FF, ff_chunk):
        u = _dot(h2, wfi_ref[:, c0:c0 + ff_chunk])
        v = _dot(h2, wfi_ref[:, D_FF + c0:D_FF + c0 + ff_chunk])
        f = f + _dot((jax.nn.silu(u) * v).astype(BF16), wfo_ref[c0:c0 + ff_chunk, :])
    x2 = x1 + _rms(f) * gfpost_ref[...]

    gate = jax.nn.sigmoid(_dot(_rms(x2).astype(BF16), wpg_ref[...]))
    o_ref[...] = x2 + gate * _dot(p_ref[...].astype(BF16), wpl_ref[...])


def _post(x2, ret2, nsa2, p2, gpre, gpost, gfpre, gfpost, gn, wg, wo, wfi, wfo, wpg, wpl, tm, ff_chunk):
    t, d = x2.shape

    def row_spec(w):
        return pl.BlockSpec((tm, w), lambda i: (i, 0))

    vec = _const_spec((1, d))
    return pl.pallas_call(
        functools.partial(_post_kernel, ff_chunk=ff_chunk),
        grid=(t // tm,),
        in_specs=[row_spec(d), row_spec(d), row_spec(d), row_spec(PLE_DIM), vec, vec, vec, vec, vec,
                  _const_spec(wg.shape), _const_spec(wo.shape), _const_spec(wfi.shape), _const_spec(wfo.shape),
                  _const_spec(wpg.shape), _const_spec(wpl.shape)],
        out_specs=row_spec(d),
        out_shape=jax.ShapeDtypeStruct((t, d), F32),
        compiler_params=pltpu.CompilerParams(dimension_semantics=("parallel",), vmem_limit_bytes=VMEM_LIMIT),
        name="post",
    )(x2, ret2, nsa2, p2, gpre, gpost, gfpre, gfpost, gn, wg, wo, wfi, wfo, wpg, wpl)


def _pad_slabs(w, n, width):
    d = w.shape[0]
    return jnp.pad(w.reshape(d, n, width), ((0, 0), (0, 0), (0, LANES - width))).reshape(d, n * LANES)


def _layer(x, p, g_mix_pre, g_mix_post, g_ffn_pre, g_ffn_post, w_in, gn_g, pe_k, w1_k, w2_k, pe_v, w1_v, w2_v,
           w_out, w_ffn_in, w_ffn_out, w_ple, w_ple_gate):
    b, s, d = x.shape
    t = b * s
    g, hpg = NSA_GROUPS, NSA_HPG
    offs = [0]
    for n in IN_SPLITS:
        offs.append(offs[-1] + n)
    (w_rq, w_rk, w_rv, w_rg, w_nq, w_kc, w_vc, w_ks, w_vs, w_kw, w_vw, w_ng, w_mg) = [
        w_in[:, offs[i]:offs[i + 1]] for i in range(len(IN_SPLITS))]

    w_a = jnp.concatenate([
        w_rq, w_rk * (RET_DK ** -0.5), w_rv,
        _pad_slabs(w_nq * (NSA_DK ** -0.5), NSA_HEADS, NSA_DK),
        _pad_slabs(w_ks, g, NSA_DK), _pad_slabs(w_kw, g, NSA_DK),
        _pad_slabs(w_vs, g, NSA_DV), _pad_slabs(w_vw, g, NSA_DV)], axis=1).astype(BF16)
    w_ng_g = w_ng.reshape(d, 3, g, hpg).transpose(0, 2, 1, 3).reshape(d, g, 3 * hpg)
    w_f = jnp.concatenate([
        w_kc, w_vc, jnp.zeros((d, PF_NG - PF_VC - g * NSA_DV), F32),
        jnp.pad(w_ng_g, ((0, 0), (0, 0), (0, LANES - 3 * hpg))).reshape(d, g * LANES)], axis=1).astype(BF16)

    x2 = x.reshape(t, d)
    gpre = g_mix_pre.reshape(1, d)
    pa = _proj(x2, gpre, w_a, BF16, 512, 512)
    pf = _proj(x2, gpre, w_f, F32, 512, PF_TOTAL)
    pa3 = pa.reshape(b, s, PA_TOTAL)
    pf3 = pf.reshape(b, s, PF_TOTAL)

    ret = _retention(pa3)

    nh = s // CMP_STRIDE
    kc_h = pf3[:, :, PF_KC:PF_KC + g * NSA_DK].reshape(b, nh, CMP_STRIDE, g, NSA_DK)
    kc_h = kc_h.transpose(0, 3, 1, 2, 4).reshape(b, g, nh, CMP_STRIDE * NSA_DK)
    vc_h = pf3[:, :, PF_VC:PF_VC + g * NSA_DV].reshape(b, nh, CMP_STRIDE, g, NSA_DV)
    vc_h = vc_h.transpose(0, 3, 1, 2, 4).reshape(b, g, nh, CMP_STRIDE * NSA_DV)
    cend = jnp.arange(nh, dtype=jnp.int32) * CMP_STRIDE + (CMP_LEN - 1)
    ab = jnp.stack([cend // LANES] * 3 + [cend % LANES] * 3, axis=1).astype(F32)
    posc = jnp.zeros((nh, LANES), F32).at[:, ALIBI_COL:ALIBI_COL + 6].set(ab)
    kcmp = _compress(kc_h, pe_k, w1_k, w2_k, posc)
    vcmp = _compress(vc_h, pe_v, w1_v, w2_v, jnp.zeros((nh, LANES), F32))

    nsa = _nsa(pa3, pf3, kcmp, vcmp)

    w_g = jnp.concatenate([w_rg, w_mg], axis=1).astype(BF16)
    out = _post(x2, ret.reshape(t, d), nsa.reshape(t, d), p.reshape(t, PLE_DIM),
                gpre, g_mix_post.reshape(1, d), g_ffn_pre.reshape(1, d), g_ffn_post.reshape(1, d),
                gn_g.reshape(1, d), w_g, w_out.astype(BF16), w_ffn_in.astype(BF16), w_ffn_out.astype(BF16),
                w_ple_gate.astype(BF16), w_ple.astype(BF16), 256, 256)
    return out.reshape(b, s, d)


def kernel(x, p, norm_mix_pre, norm_mix_post, norm_ffn_pre, norm_ffn_post, w_in, ret_gn_g, cmp_pe_k, cmp_w1_k,
           cmp_w2_k, cmp_pe_v, cmp_w1_v, cmp_w2_v, w_out, w_ffn_in, w_ffn_out, w_ple, w_ple_gate):
    for i in range(w_in.shape[0]):
        x = _layer(x, p[i], norm_mix_pre[i], norm_mix_post[i], norm_ffn_pre[i], norm_ffn_post[i], w_in[i],
                   ret_gn_g[i], cmp_pe_k[i], cmp_w1_k[i], cmp_w2_k[i], cmp_pe_v[i], cmp_w1_v[i], cmp_w2_v[i],
                   w_out[i], w_ffn_in[i], w_ffn_out[i], w_ple[i], w_ple_gate[i])
    return x
```

```python
import functools

import jax
import jax.numpy as jnp
from jax import lax
from jax.experimental import pallas as pl
from jax.experimental.pallas import tpu as pltpu

F32 = jnp.float32
BF16 = jnp.bfloat16

D_MODEL = 1024
PLE_DIM = 256
EPS = 1e-6
GN_EPS = 1e-5
NEG = -1e30
LOG2E = 1.4426950408889634

RET_HEADS = 8
RET_DV = D_MODEL // RET_HEADS
RET_DK = RET_DV // 2
RET_CHUNK = 128

NSA_HEADS = 16
NSA_GROUPS = 2
NSA_HPG = NSA_HEADS // NSA_GROUPS
NSA_DV = D_MODEL // NSA_HEADS
NSA_DK = 3 * NSA_DV // 2
CMP_LEN = 32
CMP_STRIDE = 16
SEL_LEN = 64
SEL_TOPN = 16
WINDOW = 512
D_FF = -(-8 * D_MODEL // (3 * 256)) * 256

IN_SPLITS = (
    RET_HEADS * RET_DK, RET_HEADS * RET_DK, RET_HEADS * RET_DV, RET_HEADS * RET_DV,
    NSA_HEADS * NSA_DK, NSA_GROUPS * NSA_DK, NSA_GROUPS * NSA_DV, NSA_GROUPS * NSA_DK,
    NSA_GROUPS * NSA_DV, NSA_GROUPS * NSA_DK, NSA_GROUPS * NSA_DV, 3 * NSA_HEADS, 2 * D_MODEL,
)

LANES = 128
Q_TILE = 256
SEL_KEY_TILE = 256
WIN_KEYS = WINDOW + Q_TILE
ALIBI_COL = NSA_DK
ONES_COL = NSA_DV
SUBLANES = 8
VMEM_LIMIT = 56 * 1024 * 1024

PA_RQ, PA_RK, PA_RV = 0, 512, 1024
PA_NQ = 2048
PA_KS = PA_NQ + NSA_HEADS * LANES
PA_KW = PA_KS + NSA_GROUPS * LANES
PA_VS = PA_KW + NSA_GROUPS * LANES
PA_VW = PA_VS + NSA_GROUPS * LANES
PA_TOTAL = PA_VW + NSA_GROUPS * LANES
PF_KC, PF_VC = 0, NSA_GROUPS * NSA_DK
PF_NG = 384
PF_TOTAL = PF_NG + NSA_GROUPS * LANES


def _dot(a, b):
    return jnp.dot(a, b, preferred_element_type=F32)


def _dot_nt(a, b):
    return lax.dot_general(a, b, (((1,), (1,)), ((), ())), preferred_element_type=F32)


def _dot_tn(a, b):
    return lax.dot_general(a, b, (((0,), (0,)), ((), ())), preferred_element_type=F32)


def _rms(x):
    return x * lax.rsqrt(jnp.mean(x * x, axis=-1, keepdims=True) + EPS)


def _const_spec(shape):
    nd = len(shape)
    return pl.BlockSpec(shape, lambda *_: (0,) * nd, pipeline_mode=pl.Buffered(1))


def _proj_kernel(x_ref, g_ref, w_ref, o_ref, *, chunk):
    h = (_rms(x_ref[...]) * g_ref[...]).astype(BF16)
    n = o_ref.shape[-1]
    for c0 in range(0, n, chunk):
        c1 = min(c0 + chunk, n)
        o_ref[:, c0:c1] = _dot(h, w_ref[:, c0:c1]).astype(o_ref.dtype)


def _proj(x2, g, w, out_dtype, tm, chunk):
    t, d = x2.shape
    n = w.shape[1]
    return pl.pallas_call(
        functools.partial(_proj_kernel, chunk=chunk),
        grid=(t // tm,),
        in_specs=[pl.BlockSpec((tm, d), lambda i: (i, 0)), _const_spec((1, d)), _const_spec((d, n))],
        out_specs=pl.BlockSpec((tm, n), lambda i: (i, 0)),
        out_shape=jax.ShapeDtypeStruct((t, n), out_dtype),
        compiler_params=pltpu.CompilerParams(dimension_semantics=("parallel",), vmem_limit_bytes=VMEM_LIMIT),
        name="proj_" + jnp.dtype(out_dtype).name,
    )(x2, g, w)


def _ret_kernel(q_ref, k_ref, v_ref, dmat_ref, xi_ref, zeta_ref, dec_ref, o_ref, state_ref):
    @pl.when(pl.program_id(1) == 0)
    def _():
        state_ref[...] = jnp.zeros_like(state_ref)

    for h in range(RET_HEADS):
        q = q_ref[0, :, h * RET_DK:(h + 1) * RET_DK]
        k = k_ref[0, :, h * RET_DK:(h + 1) * RET_DK]
        v = v_ref[0, :, h * RET_DV:(h + 1) * RET_DV]
        r = state_ref[h]
        inner = _dot_nt(q, k) * dmat_ref[h]
        o = _dot(inner.astype(BF16), v) + _dot(q, r.astype(BF16)) * xi_ref[h]
        kz = (k.astype(F32) * zeta_ref[h]).astype(BF16)
        state_ref[h] = r * dec_ref[h] + _dot_tn(kz, v)
        mu = jnp.mean(o, axis=-1, keepdims=True)
        d = o - mu
        var = jnp.mean(d * d, axis=-1, keepdims=True)
        o_ref[0, :, h * RET_DV:(h + 1) * RET_DV] = (d * lax.rsqrt(var + GN_EPS)).astype(o_ref.dtype)


def _retention(pa3):
    b, s, _ = pa3.shape
    c = RET_CHUNK
    hh = RET_HEADS
    log_g = jnp.log1p(-(2.0 ** (-5.0 - jnp.arange(hh, dtype=F32))))
    idx = jnp.arange(c, dtype=F32)
    diff = idx[:, None] - idx[None, :]
    dmat = jnp.exp(log_g[:, None, None] * jnp.maximum(diff, 0.0)) * (diff >= 0)
    xi = jnp.broadcast_to(jnp.exp(log_g[:, None] * (idx + 1.0))[:, :, None], (hh, c, RET_DV))
    zeta = jnp.broadcast_to(jnp.exp(log_g[:, None] * (c - 1.0 - idx))[:, :, None], (hh, c, RET_DK))
    dec = jnp.broadcast_to(jnp.exp(log_g * c)[:, None, None], (hh, RET_DK, RET_DV))
    wq = hh * RET_DK
    wv = hh * RET_DV
    return pl.pallas_call(
        _ret_kernel,
        grid=(b, s // c),
        in_specs=[
            pl.BlockSpec((1, c, wq), lambda i, j: (i, j, PA_RQ // wq)),
            pl.BlockSpec((1, c, wq), lambda i, j: (i, j, PA_RK // wq)),
            pl.BlockSpec((1, c, wv), lambda i, j: (i, j, PA_RV // wv)),
            _const_spec((hh, c, c)), _const_spec((hh, c, RET_DV)), _const_spec((hh, c, RET_DK)),
            _const_spec((hh, RET_DK, RET_DV)),
        ],
        out_specs=pl.BlockSpec((1, c, wv), lambda i, j: (i, j, 0)),
        out_shape=jax.ShapeDtypeStruct((b, s, wv), BF16),
        scratch_shapes=[pltpu.VMEM((hh, RET_DK, RET_DV), F32)],
        compiler_params=pltpu.CompilerParams(dimension_semantics=("parallel", "arbitrary")),
        name="retention",
    )(pa3, pa3, pa3, dmat, xi, zeta, dec)


def _cmp_kernel(h_ref, pe_top_ref, pe_bot_ref, w1t_ref, w1b_ref, w2_ref, tab_ref, o_ref):
    hb = h_ref[0, 0]
    top = _dot((hb + pe_top_ref[...]).astype(BF16), w1t_ref[...])
    bot = _dot((hb + pe_bot_ref[...]).astype(BF16), w1b_ref[...])
    n = bot.shape[0]
    pre = top + pltpu.roll(bot, n - 1, 0)
    act = jax.nn.gelu(pre)
    o_ref[0, 0] = (_dot(act.astype(BF16), w2_ref[...]) + tab_ref[...]).astype(o_ref.dtype)


def _compress(hblk, pe, w1, w2, tab):
    b, g, n, wd = hblk.shape
    d = wd // CMP_STRIDE
    pe_top = pe[:CMP_STRIDE].reshape(1, wd)
    pe_bot = pe[CMP_STRIDE:].reshape(1, wd)
    w1t = w1[:wd].astype(BF16)
    w1b = w1[wd:].astype(BF16)
    w2p = jnp.pad(w2, ((0, 0), (0, LANES - w2.shape[1]))).astype(BF16)
    return pl.pallas_call(
        _cmp_kernel,
        grid=(b, g),
        in_specs=[pl.BlockSpec((1, 1, n, wd), lambda i, j: (i, j, 0, 0)),
                  _const_spec((1, wd)), _const_spec((1, wd)), _const_spec((wd, d)), _const_spec((wd, d)),
                  _const_spec((d, LANES)), _const_spec((n, LANES))],
        out_specs=pl.BlockSpec((1, 1, n, LANES), lambda i, j: (i, j, 0, 0)),
        out_shape=jax.ShapeDtypeStruct((b, g, n, LANES), BF16),
        compiler_params=pltpu.CompilerParams(dimension_semantics=("parallel", "parallel")),
        name="compress_%d" % d,
    )(hblk, pe_top, pe_bot, w1t, w1b, w2p, tab)


def _split3(x):
    p1 = x.astype(BF16)
    r1 = x - p1.astype(F32)
    p2 = r1.astype(BF16)
    p3 = (r1 - p2.astype(F32)).astype(BF16)
    return p1, p2, p3


def _nsa_kernel(q_ref, ng_ref, kc_ref, vc_ref, ks_ref, kw_ref, vs_ref, vw_ref,
                posk_ref, oh_ref, qal_ref, ovt_ref, ones_ref, o_ref, m_ref, acc_ref, s_ref, mx_ref, part_ref, *, topn):
    hpg = NSA_HPG
    tq = Q_TILE
    rows = hpg * tq
    t0 = pl.program_id(2) * tq
    n_sel = ovt_ref.shape[0]
    n_cmp = kc_ref.shape[2]

    q_all = q_ref[0]
    q128 = jnp.concatenate([q_all[:, h * LANES:(h + 1) * LANES] for h in range(hpg)], axis=0) + qal_ref[0]

    s_c = _dot_nt(q128, kc_ref[0, 0]).reshape(hpg, tq, n_cmp)
    t_c = t0 + lax.broadcasted_iota(jnp.int32, (tq, n_cmp), 0)
    end_c = lax.broadcasted_iota(jnp.int32, (tq, n_cmp), 1) * CMP_STRIDE + (CMP_LEN - 1)
    s_c = s_c + jnp.where(end_c <= t_c, 0.0, NEG)[None]
    e_c = jnp.exp2(s_c - jnp.max(s_c, axis=-1, keepdims=True))
    acc_c = _dot(e_c.reshape(rows, n_cmp).astype(BF16), vc_ref[0, 0] + ones_ref[0:n_cmp, :])
    seen = (t0 + lax.broadcasted_iota(jnp.int32, (tq, 1), 0)) >= CMP_LEN - 1
    inv_l = jnp.where(seen[None], 1.0 / jnp.sum(e_c, axis=-1, keepdims=True), 0.0)

    psum = jnp.sum(e_c * inv_l, axis=0)
    ovt = ovt_ref[...]
    imp_t = sum(_dot_nt(ovt, piece) for piece in _split3(psum))
    jj = lax.broadcasted_iota(jnp.int32, (n_sel, tq), 0)
    tt = t0 + lax.broadcasted_iota(jnp.int32, (n_sel, tq), 1)
    cur = tt // SEL_LEN
    forced = (jj == 0) | (jj == cur) | (jj == cur - 1)
    score = jnp.where(forced, 1e6, jnp.where(jj * SEL_LEN <= tt, imp_t, -1e6))
    sub = lax.broadcasted_iota(jnp.int32, (SUBLANES, tq), 0)
    groups = [score[SUBLANES * v:SUBLANES * (v + 1), :] for v in range(n_sel // SUBLANES)]
    ranks = [jnp.zeros((SUBLANES, tq), F32) for _ in groups]
    for i in range(n_sel):
        vi, ri = divmod(i, SUBLANES)
        si = jnp.broadcast_to(groups[vi][ri:ri + 1, :], (SUBLANES, tq))
        for v, sv in enumerate(groups):
            if v > vi:
                before = jnp.where(si >= sv, 1.0, 0.0)
            elif v < vi:
                before = jnp.where(si > sv, 1.0, 0.0)
            else:
                before = jnp.where(sub > ri, jnp.where(si >= sv, 1.0, 0.0), jnp.where(si > sv, 1.0, 0.0))
            ranks[v] = ranks[v] + before
    pen_t = jnp.where(jnp.concatenate(ranks, axis=0) < float(topn), 0.0, NEG)
    pen_t = jnp.concatenate([pen_t, jnp.zeros((LANES - n_sel, tq), F32)], axis=0)
    pen = pen_t.T.astype(BF16)
    q256 = jnp.concatenate([q128, jnp.concatenate([pen] * hpg, axis=0)], axis=1)

    gate = jax.nn.sigmoid(ng_ref[0])
    lane = lax.broadcasted_iota(jnp.int32, (tq, LANES), 1)
    low = lane < NSA_DV

    def gated(acc, branch):
        outs = []
        for hp in range(hpg // 2):
            a_e = acc[(2 * hp) * tq:(2 * hp + 1) * tq]
            a_o = acc[(2 * hp + 1) * tq:(2 * hp + 2) * tq]
            keep = jnp.where(low, a_e, a_o)
            swap = pltpu.roll(jnp.where(low, a_o, a_e), NSA_DV, 1)
            num = jnp.where(low, keep, swap)
            den = jnp.where(low, swap, keep)
            col = branch * hpg + 2 * hp
            g2 = jnp.where(low, gate[:, col:col + 1], gate[:, col + 1:col + 2])
            outs.append(g2 * num / den)
        return jnp.concatenate(outs, axis=1)

    w0 = pl.multiple_of(jnp.maximum(t0 - WINDOW, 0), LANES)
    kw = kw_ref[0, pl.ds(w0, WIN_KEYS), :] + posk_ref[pl.ds(w0, WIN_KEYS), :]
    s_w = _dot_nt(q128, kw).reshape(hpg, tq, WIN_KEYS)
    pos_w = w0 + lax.broadcasted_iota(jnp.int32, (tq, WIN_KEYS), 1)
    t_w = t0 + lax.broadcasted_iota(jnp.int32, (tq, WIN_KEYS), 0)
    s_w = s_w + jnp.where((pos_w <= t_w) & (pos_w > t_w - WINDOW), 0.0, NEG)[None]
    e_w = jnp.exp2(s_w - jnp.max(s_w, axis=-1, keepdims=True))
    acc_w = _dot(e_w.reshape(rows, WIN_KEYS).astype(BF16), vw_ref[0, pl.ds(w0, WIN_KEYS), :] + ones_ref[...])
    part_ref[...] = jnp.where(seen, gated(acc_c, 0), 0.0) + gated(acc_w, 2)

    tk = SEL_KEY_TILE
    ones_v = ones_ref[0:tk, :]
    n_full = t0 // tk

    def logits(kt):
        k0 = pl.multiple_of(kt * tk, tk)
        k256 = jnp.concatenate([ks_ref[0, pl.ds(k0, tk), :] + posk_ref[pl.ds(k0, tk), :],
                                oh_ref[pl.ds(k0, tk), :]], axis=1)
        return _dot_nt(q256, k256)

    def values(kt):
        return vs_ref[0, pl.ds(pl.multiple_of(kt * tk, tk), tk), :] + ones_v

    pos_d = n_full * tk + lax.broadcasted_iota(jnp.int32, (tq, tk), 1)
    t_d = t0 + lax.broadcasted_iota(jnp.int32, (tq, tk), 0)
    s_d = (logits(n_full).reshape(hpg, tq, tk) + jnp.where(pos_d <= t_d, 0.0, NEG)[None]).reshape(rows, tk)
    m_d = jnp.broadcast_to(jnp.max(s_d, axis=-1, keepdims=True), (rows, LANES))
    m_ref[...] = m_d
    p_d = jnp.exp2(s_d - jnp.concatenate([m_d] * (tk // LANES), axis=1))
    acc_ref[...] = _dot(p_d.astype(BF16), values(n_full))

    def row_max(s):
        return jnp.broadcast_to(jnp.max(s, axis=-1, keepdims=True), (rows, LANES))

    s_first = logits(0)
    s_ref[0] = s_first
    mx_ref[0] = row_max(s_first)
    s_ref[1] = logits(jnp.minimum(1, n_full))

    def body(kt, carry):
        s = s_ref[kt % 3]
        m_cur = mx_ref[kt % 2]
        mx_next = row_max(s_ref[(kt + 1) % 3])
        s_next = logits(jnp.minimum(kt + 2, n_full))
        m_prev = m_ref[...]
        m_new = jnp.maximum(m_prev, m_cur)
        p = jnp.exp2(s - jnp.concatenate([m_new] * (tk // LANES), axis=1))
        acc_ref[...] = acc_ref[...] * jnp.exp2(m_prev - m_new) + _dot(p.astype(BF16), values(kt))
        m_ref[...] = m_new
        mx_ref[(kt + 1) % 2] = mx_next
        s_ref[(kt + 2) % 3] = s_next
        return carry

    lax.fori_loop(0, n_full, body, 0)
    o_ref[0] = (part_ref[...] + gated(acc_ref[...], 1)).astype(o_ref.dtype)


def _nsa(pa3, pf3, kcmp, vcmp):
    b, s, _ = pa3.shape
    g, hpg = NSA_GROUPS, NSA_HPG
    n_sel = s // SEL_LEN
    n_cmp = kcmp.shape[2]
    topn = min(SEL_TOPN, n_sel)
    rows = hpg * Q_TILE

    pos = jnp.arange(s, dtype=jnp.int32)
    ab = jnp.stack([pos // LANES] * 3 + [pos % LANES] * 3, axis=1).astype(F32)
    posk = jnp.zeros((s, LANES), F32).at[:, ALIBI_COL:ALIBI_COL + 6].set(ab).astype(BF16)
    onehot = (pos[:, None] // SEL_LEN == jnp.arange(LANES)[None, :]).astype(BF16)
    slopes = LOG2E * 2.0 ** (-8.0 * (jnp.arange(NSA_HEADS, dtype=F32) + 1.0) / NSA_HEADS)
    s1, s2, s3 = _split3(slopes)
    qcols = jnp.stack([s1.astype(F32) * LANES, s2.astype(F32) * LANES, s3.astype(F32) * LANES,
                       s1.astype(F32), s2.astype(F32), s3.astype(F32)], axis=1)
    qal = jnp.zeros((NSA_HEADS, LANES), F32).at[:, ALIBI_COL:ALIBI_COL + 6].set(qcols)
    qal = jnp.broadcast_to(qal.reshape(g, hpg, 1, LANES), (g, hpg, Q_TILE, LANES)).reshape(g, rows, LANES)
    qal = qal.astype(BF16)
    cstart = jnp.arange(n_cmp) * CMP_STRIDE
    sstart = jnp.arange(n_sel) * SEL_LEN
    ovt = ((cstart[None, :] < sstart[:, None] + SEL_LEN) & (cstart[None, :] + CMP_LEN > sstart[:, None])
           & (jnp.arange(n_cmp)[None, :] < n_cmp - 1)).astype(BF16)
    ones = jnp.zeros((WIN_KEYS, LANES), BF16).at[:, ONES_COL:].set(1.0)

    def seq_spec(col0):
        return pl.BlockSpec((1, s, LANES), lambda i, j, t: (i, 0, col0 // LANES + j))

    return pl.pallas_call(
        functools.partial(_nsa_kernel, topn=topn),
        grid=(b, g, s // Q_TILE),
        in_specs=[
            pl.BlockSpec((1, Q_TILE, hpg * LANES), lambda i, j, t: (i, t, PA_NQ // (hpg * LANES) + j)),
            pl.BlockSpec((1, Q_TILE, LANES), lambda i, j, t: (i, t, PF_NG // LANES + j)),
            pl.BlockSpec((1, 1, n_cmp, LANES), lambda i, j, t: (i, j, 0, 0)),
            pl.BlockSpec((1, 1, n_cmp, LANES), lambda i, j, t: (i, j, 0, 0)),
            seq_spec(PA_KS), seq_spec(PA_KW), seq_spec(PA_VS), seq_spec(PA_VW),
            _const_spec((s, LANES)), _const_spec((s, LANES)),
            pl.BlockSpec((1, rows, LANES), lambda i, j, t: (j, 0, 0)),
            _const_spec((n_sel, n_cmp)), _const_spec((WIN_KEYS, LANES)),
        ],
        out_specs=pl.BlockSpec((1, Q_TILE, hpg * NSA_DV), lambda i, j, t: (i, t, j)),
        out_shape=jax.ShapeDtypeStruct((b, s, NSA_HEADS * NSA_DV), BF16),
        scratch_shapes=[pltpu.VMEM((rows, LANES), F32), pltpu.VMEM((rows, LANES), F32),
                        pltpu.VMEM((3, rows, SEL_KEY_TILE), F32), pltpu.VMEM((2, rows, LANES), F32),
                        pltpu.VMEM((Q_TILE, hpg * NSA_DV), F32)],
        compiler_params=pltpu.CompilerParams(dimension_semantics=("parallel", "parallel", "parallel"),
                                             vmem_limit_bytes=VMEM_LIMIT),
        name="nsa_attention",
    )(pa3, pf3, kcmp, vcmp, pa3, pa3, pa3, pa3, posk, onehot, qal, ovt, ones)


def _post_kernel(x_ref, ret_ref, nsa_ref, p_ref, gpre_ref, gpost_ref, gfpre_ref, gfpost_ref, gn_ref,
                 wg_ref, wo_ref, wfi_ref, wfo_ref, wpg_ref, wpl_ref, o_ref, *, ff_chunk):
    d = D_MODEL
    x = x_ref[...]
    h = (_rms(x) * gpre_ref[...]).astype(BF16)
    ret = jax.nn.silu(_dot(h, wg_ref[:, 0:d])) * (ret_ref[...].astype(F32) * gn_ref[...])
    mixin = jax.nn.sigmoid(_dot(h, wg_ref[:, d:2 * d])) * ret
    mixin = mixin + jax.nn.sigmoid(_dot(h, wg_ref[:, 2 * d:3 * d])) * nsa_ref[...].astype(F32)
    x1 = x + _rms(_dot(mixin.astype(BF16), wo_ref[...])) * gpost_ref[...]

    h2 = (_rms(x1) * gfpre_ref[...]).astype(BF16)
    f = jnp.zeros(x.shape, F32)
    for c0 in range(0, D_FF, ff_chunk):
        u = _dot(h2, wfi_ref[:, c0:c0 + ff_chunk])
        v = _dot(h2, wfi_ref[:, D_FF + c0:D_FF + c0 + ff_chunk])
        f = f + _dot((jax.nn.silu(u) * v).astype(BF16), wfo_ref[c0:c0 + ff_chunk, :])
    x2 = x1 + _rms(f) * gfpost_ref[...]

    gate = jax.nn.sigmoid(_dot(_rms(x2).astype(BF16), wpg_ref[...]))
    o_ref[...] = x2 + gate * _dot(p_ref[...].astype(BF16), wpl_ref[...])


def _post(x2, ret2, nsa2, p2, gpre, gpost, gfpre, gfpost, gn, wg, wo, wfi, wfo, wpg, wpl, tm, ff_chunk):
    t, d = x2.shape

    def row_spec(w):
        return pl.BlockSpec((tm, w), lambda i: (i, 0))

    vec = _const_spec((1, d))
    return pl.pallas_call(
        functools.partial(_post_kernel, ff_chunk=ff_chunk),
        grid=(t // tm,),
        in_specs=[row_spec(d), row_spec(d), row_spec(d), row_spec(PLE_DIM), vec, vec, vec, vec, vec,
                  _const_spec(wg.shape), _const_spec(wo.shape), _const_spec(wfi.shape), _const_spec(wfo.shape),
                  _const_spec(wpg.shape), _const_spec(wpl.shape)],
        out_specs=row_spec(d),
        out_shape=jax.ShapeDtypeStruct((t, d), F32),
        compiler_params=pltpu.CompilerParams(dimension_semantics=("parallel",), vmem_limit_bytes=VMEM_LIMIT),
        name="post",
    )(x2, ret2, nsa2, p2, gpre, gpost, gfpre, gfpost, gn, wg, wo, wfi, wfo, wpg, wpl)


def _pad_slabs(w, n, width):
    d = w.shape[0]
    return jnp.pad(w.reshape(d, n, width), ((0, 0), (0, 0), (0, LANES - width))).reshape(d, n * LANES)


def _layer(x, p, g_mix_pre, g_mix_post, g_ffn_pre, g_ffn_post, w_in, gn_g, pe_k, w1_k, w2_k, pe_v, w1_v, w2_v,
           w_out, w_ffn_in, w_ffn_out, w_ple, w_ple_gate):
    b, s, d = x.shape
    t = b * s
    g, hpg = NSA_GROUPS, NSA_HPG
    offs = [0]
    for n in IN_SPLITS:
        offs.append(offs[-1] + n)
    (w_rq, w_rk, w_rv, w_rg, w_nq, w_kc, w_vc, w_ks, w_vs, w_kw, w_vw, w_ng, w_mg) = [
        w_in[:, offs[i]:offs[i + 1]] for i in range(len(IN_SPLITS))]

    w_a = jnp.concatenate([
        w_rq, w_rk * (RET_DK ** -0.5), w_rv,
        _pad_slabs(w_nq * (LOG2E * NSA_DK ** -0.5), NSA_HEADS, NSA_DK),
        _pad_slabs(w_ks, g, NSA_DK), _pad_slabs(w_kw, g, NSA_DK),
        _pad_slabs(w_vs, g, NSA_DV), _pad_slabs(w_vw, g, NSA_DV)], axis=1).astype(BF16)
    w_ng_g = w_ng.reshape(d, 3, g, hpg).transpose(0, 2, 1, 3).reshape(d, g, 3 * hpg)
    w_f = jnp.concatenate([
        w_kc, w_vc, jnp.zeros((d, PF_NG - PF_VC - g * NSA_DV), F32),
        jnp.pad(w_ng_g, ((0, 0), (0, 0), (0, LANES - 3 * hpg))).reshape(d, g * LANES)], axis=1).astype(BF16)

    x2 = x.reshape(t, d)
    gpre = g_mix_pre.reshape(1, d)
    pa = _proj(x2, gpre, w_a, BF16, 512, 512)
    pf = _proj(x2, gpre, w_f, F32, 512, PF_TOTAL)
    pa3 = pa.reshape(b, s, PA_TOTAL)
    pf3 = pf.reshape(b, s, PF_TOTAL)

    ret = _retention(pa3)

    nh = s // CMP_STRIDE
    kc_h = pf3[:, :, PF_KC:PF_KC + g * NSA_DK].reshape(b, nh, CMP_STRIDE, g, NSA_DK)
    kc_h = kc_h.transpose(0, 3, 1, 2, 4).reshape(b, g, nh, CMP_STRIDE * NSA_DK)
    vc_h = pf3[:, :, PF_VC:PF_VC + g * NSA_DV].reshape(b, nh, CMP_STRIDE, g, NSA_DV)
    vc_h = vc_h.transpose(0, 3, 1, 2, 4).reshape(b, g, nh, CMP_STRIDE * NSA_DV)
    cend = jnp.arange(nh, dtype=jnp.int32) * CMP_STRIDE + (CMP_LEN - 1)
    ab = jnp.stack([cend // LANES] * 3 + [cend % LANES] * 3, axis=1).astype(F32)
    posc = jnp.zeros((nh, LANES), F32).at[:, ALIBI_COL:ALIBI_COL + 6].set(ab)
    kcmp = _compress(kc_h, pe_k, w1_k, w2_k, posc)
    vcmp = _compress(vc_h, pe_v, w1_v, w2_v, jnp.zeros((nh, LANES), F32))

    nsa = _nsa(pa3, pf3, kcmp, vcmp)

    w_g = jnp.concatenate([w_rg, w_mg], axis=1).astype(BF16)
    out = _post(x2, ret.reshape(t, d), nsa.reshape(t, d), p.reshape(t, PLE_DIM),
                gpre, g_mix_post.reshape(1, d), g_ffn_pre.reshape(1, d), g_ffn_post.reshape(1, d),
                gn_g.reshape(1, d), w_g, w_out.astype(BF16), w_ffn_in.astype(BF16), w_ffn_out.astype(BF16),
                w_ple_gate.astype(BF16), w_ple.astype(BF16), 256, 256)
    return out.reshape(b, s, d)


def kernel(x, p, norm_mix_pre, norm_mix_post, norm_ffn_pre, norm_ffn_post, w_in, ret_gn_g, cmp_pe_k, cmp_w1_k,
           cmp_w2_k, cmp_pe_v, cmp_w1_v, cmp_w2_v, w_out, w_ffn_in, w_ffn_out, w_ple, w_ple_gate):
    for i in range(w_in.shape[0]):
        x = _layer(x, p[i], norm_mix_pre[i], norm_mix_post[i], norm_ffn_pre[i], norm_ffn_post[i], w_in[i],
                   ret_gn_g[i], cmp_pe_k[i], cmp_w1_k[i], cmp_w2_k[i], cmp_pe_v[i], cmp_w1_v[i], cmp_w2_v[i],
                   w_out[i], w_ffn_in[i], w_ffn_out[i], w_ple[i], w_ple_gate[i])
    return x
```

```python
import functools

import jax
import jax.numpy as jnp
from jax import lax
from jax.experimental import pallas as pl
from jax.experimental.pallas import tpu as pltpu

F32 = jnp.float32
BF16 = jnp.bfloat16

D_MODEL = 1024
PLE_DIM = 256
EPS = 1e-6
GN_EPS = 1e-5
NEG = -1e30
LOG2E = 1.4426950408889634

RET_HEADS = 8
RET_DV = D_MODEL // RET_HEADS
RET_DK = RET_DV // 2
RET_CHUNK = 128

NSA_HEADS = 16
NSA_GROUPS = 2
NSA_HPG = NSA_HEADS // NSA_GROUPS
NSA_DV = D_MODEL // NSA_HEADS
NSA_DK = 3 * NSA_DV // 2
CMP_LEN = 32
CMP_STRIDE = 16
SEL_LEN = 64
SEL_TOPN = 16
WINDOW = 512
D_FF = -(-8 * D_MODEL // (3 * 256)) * 256

IN_SPLITS = (
    RET_HEADS * RET_DK, RET_HEADS * RET_DK, RET_HEADS * RET_DV, RET_HEADS * RET_DV,
    NSA_HEADS * NSA_DK, NSA_GROUPS * NSA_DK, NSA_GROUPS * NSA_DV, NSA_GROUPS * NSA_DK,
    NSA_GROUPS * NSA_DV, NSA_GROUPS * NSA_DK, NSA_GROUPS * NSA_DV, 3 * NSA_HEADS, 2 * D_MODEL,
)

LANES = 128
Q_TILE = 256
SEL_KEY_TILE = 256
WIN_KEYS = WINDOW + Q_TILE
ALIBI_COL = NSA_DK
ONES_COL = NSA_DV
SUBLANES = 8
PAIR = LANES // NSA_DV
VMEM_LIMIT = 56 * 1024 * 1024

PA_RQ, PA_RK, PA_RV = 0, 512, 1024
PA_NQ = 2048
PA_KS = PA_NQ + NSA_HEADS * LANES
PA_KW = PA_KS + NSA_GROUPS * LANES
PA_VS = PA_KW + NSA_GROUPS * LANES
PA_VW = PA_VS + NSA_GROUPS * LANES
PA_TOTAL = PA_VW + NSA_GROUPS * LANES
PF_KC = 0
PF_VC = PF_KC + NSA_GROUPS * LANES
PF_NG = PF_VC + LANES
PF_TOTAL = PF_NG + NSA_GROUPS * LANES


def _dot(a, b):
    return jnp.dot(a, b, preferred_element_type=F32)


def _dot_nt(a, b):
    return lax.dot_general(a, b, (((1,), (1,)), ((), ())), preferred_element_type=F32)


def _dot_tn(a, b):
    return lax.dot_general(a, b, (((0,), (0,)), ((), ())), preferred_element_type=F32)


def _rms(x):
    return x * lax.rsqrt(jnp.mean(x * x, axis=-1, keepdims=True) + EPS)


def _const_spec(shape):
    nd = len(shape)
    return pl.BlockSpec(shape, lambda *_: (0,) * nd, pipeline_mode=pl.Buffered(1))


def _proj_kernel(x_ref, g_ref, wa_ref, wf_ref, oa_ref, of_ref, *, chunk):
    h = (_rms(x_ref[...]) * g_ref[...]).astype(BF16)
    for w_ref, o_ref in ((wa_ref, oa_ref), (wf_ref, of_ref)):
        n = o_ref.shape[-1]
        for c0 in range(0, n, chunk):
            c1 = min(c0 + chunk, n)
            o_ref[:, c0:c1] = _dot(h, w_ref[:, c0:c1]).astype(o_ref.dtype)


def _proj(x2, g, w_a, w_f, tm, chunk):
    t, d = x2.shape
    na, nf = w_a.shape[1], w_f.shape[1]
    return pl.pallas_call(
        functools.partial(_proj_kernel, chunk=chunk),
        grid=(t // tm,),
        in_specs=[pl.BlockSpec((tm, d), lambda i: (i, 0)), _const_spec((1, d)), _const_spec((d, na)),
                  _const_spec((d, nf))],
        out_specs=[pl.BlockSpec((tm, na), lambda i: (i, 0)), pl.BlockSpec((tm, nf), lambda i: (i, 0))],
        out_shape=[jax.ShapeDtypeStruct((t, na), BF16), jax.ShapeDtypeStruct((t, nf), F32)],
        compiler_params=pltpu.CompilerParams(dimension_semantics=("parallel",), vmem_limit_bytes=VMEM_LIMIT),
        name="proj",
    )(x2, g, w_a, w_f)


def _ret_kernel(q_ref, k_ref, v_ref, dmat_ref, xi_ref, zeta_ref, dec_ref, o_ref, state_ref):
    @pl.when(pl.program_id(1) == 0)
    def _():
        state_ref[...] = jnp.zeros_like(state_ref)

    hs = range(RET_HEADS)
    q = [q_ref[0, :, h * RET_DK:(h + 1) * RET_DK] for h in hs]
    k = [k_ref[0, :, h * RET_DK:(h + 1) * RET_DK] for h in hs]
    v = [v_ref[0, :, h * RET_DV:(h + 1) * RET_DV] for h in hs]
    r = [state_ref[h] for h in hs]
    inner = [_dot_nt(q[h], k[h]) for h in hs]
    cross = [_dot(q[h], r[h].astype(BF16)) for h in hs]
    kz = [(k[h].astype(F32) * zeta_ref[h]).astype(BF16) for h in hs]
    upd = [_dot_tn(kz[h], v[h]) for h in hs]
    inner = [(inner[h] * dmat_ref[h]).astype(BF16) for h in hs]
    o = [_dot(inner[h], v[h]) + cross[h] * xi_ref[h] for h in hs]
    for h in hs:
        state_ref[h] = r[h] * dec_ref[h] + upd[h]
    mu = [jnp.mean(o[h], axis=-1, keepdims=True) for h in hs]
    d = [o[h] - mu[h] for h in hs]
    var = [jnp.mean(d[h] * d[h], axis=-1, keepdims=True) for h in hs]
    for h in hs:
        o_ref[0, :, h * RET_DV:(h + 1) * RET_DV] = (d[h] * lax.rsqrt(var[h] + GN_EPS)).astype(o_ref.dtype)


def _retention(pa3):
    b, s, _ = pa3.shape
    c = RET_CHUNK
    hh = RET_HEADS
    log_g = jnp.log1p(-(2.0 ** (-5.0 - jnp.arange(hh, dtype=F32))))
    idx = jnp.arange(c, dtype=F32)
    diff = idx[:, None] - idx[None, :]
    dmat = jnp.exp(log_g[:, None, None] * jnp.maximum(diff, 0.0)) * (diff >= 0)
    xi = jnp.broadcast_to(jnp.exp(log_g[:, None] * (idx + 1.0))[:, :, None], (hh, c, RET_DV))
    zeta = jnp.broadcast_to(jnp.exp(log_g[:, None] * (c - 1.0 - idx))[:, :, None], (hh, c, RET_DK))
    dec = jnp.broadcast_to(jnp.exp(log_g * c)[:, None, None], (hh, RET_DK, RET_DV))
    wq = hh * RET_DK
    wv = hh * RET_DV
    return pl.pallas_call(
        _ret_kernel,
        grid=(b, s // c),
        in_specs=[
            pl.BlockSpec((1, c, wq), lambda i, j: (i, j, PA_RQ // wq)),
            pl.BlockSpec((1, c, wq), lambda i, j: (i, j, PA_RK // wq)),
            pl.BlockSpec((1, c, wv), lambda i, j: (i, j, PA_RV // wv)),
            _const_spec((hh, c, c)), _const_spec((hh, c, RET_DV)), _const_spec((hh, c, RET_DK)),
            _const_spec((hh, RET_DK, RET_DV)),
        ],
        out_specs=pl.BlockSpec((1, c, wv), lambda i, j: (i, j, 0)),
        out_shape=jax.ShapeDtypeStruct((b, s, wv), BF16),
        scratch_shapes=[pltpu.VMEM((hh, RET_DK, RET_DV), F32)],
        compiler_params=pltpu.CompilerParams(dimension_semantics=("parallel", "arbitrary")),
        name="retention",
    )(pa3, pa3, pa3, dmat, xi, zeta, dec)


def _cmp_kernel(src_ref, pe_ref, w1_ref, w2_ref, tab_ref, o_ref):
    n = o_ref.shape[2]
    top = jnp.zeros((n, LANES), F32)
    bot = jnp.zeros((n, LANES), F32)
    for l in range(CMP_STRIDE):
        x = src_ref[0, pl.ds(l, n, stride=CMP_STRIDE), :]
        top = top + _dot((x + pe_ref[0, l:l + 1, :]).astype(BF16), w1_ref[0, l])
        bot = bot + _dot((x + pe_ref[0, CMP_STRIDE + l:CMP_STRIDE + l + 1, :]).astype(BF16),
                         w1_ref[0, CMP_STRIDE + l])
    pre = top + pltpu.roll(bot, n - 1, 0)
    act = jax.nn.gelu(pre)
    o_ref[0, 0] = (_dot(act.astype(BF16), w2_ref[...]) + tab_ref[...]).astype(o_ref.dtype)


def _compress(pf3, slab0, slab_step, pe, w1, w2, tab):
    b, s, _ = pf3.shape
    g = NSA_GROUPS
    n = s // CMP_STRIDE
    d = w2.shape[0]
    lane_step = 0 if slab_step else d
    pe_g = jnp.stack([jnp.pad(pe, ((0, 0), (j * lane_step, LANES - d - j * lane_step))) for j in range(g)])
    w1_g = jnp.stack([jnp.pad(w1.reshape(CMP_LEN, d, d), ((0, 0), (j * lane_step, LANES - d - j * lane_step),
                                                            (0, LANES - d))) for j in range(g)]).astype(BF16)
    w2p = jnp.pad(w2, ((0, LANES - d), (0, LANES - d))).astype(BF16)
    return pl.pallas_call(
        _cmp_kernel,
        grid=(b, g),
        in_specs=[pl.BlockSpec((1, s, LANES), lambda i, j: (i, 0, slab0 + j * slab_step)),
                  pl.BlockSpec((1, CMP_LEN, LANES), lambda i, j: (j, 0, 0)),
                  pl.BlockSpec((1, CMP_LEN, LANES, LANES), lambda i, j: (j, 0, 0, 0)),
                  _const_spec((LANES, LANES)), _const_spec((n, LANES))],
        out_specs=pl.BlockSpec((1, 1, n, LANES), lambda i, j: (i, j, 0, 0)),
        out_shape=jax.ShapeDtypeStruct((b, g, n, LANES), BF16),
        compiler_params=pltpu.CompilerParams(dimension_semantics=("parallel", "parallel")),
        name="compress_%d" % d,
    )(pf3, pe_g, w1_g, w2p, tab)


def _split3(x):
    p1 = x.astype(BF16)
    r1 = x - p1.astype(F32)
    p2 = r1.astype(BF16)
    p3 = (r1 - p2.astype(F32)).astype(BF16)
    return p1, p2, p3


def _nsa_kernel(q_ref, ng_ref, kc_ref, vc_ref, ks_ref, kw_ref, vs_ref, vw_ref,
                posk_ref, oh_ref, qal_ref, ovt_ref, ones_ref, o_ref, m_ref, acc_ref, s_ref, mx_ref, part_ref, *, topn):
    hpg = NSA_HPG
    tq = Q_TILE
    rows = hpg * tq
    t0 = pl.program_id(2) * tq
    n_sel = ovt_ref.shape[0]
    n_cmp = kc_ref.shape[2]

    q_all = q_ref[0]
    q128 = jnp.concatenate([q_all[:, h * LANES:(h + 1) * LANES] for h in range(hpg)], axis=0) + qal_ref[0]

    pair_rows = PAIR * tq
    n_pairs = hpg // PAIR

    t_c = t0 + lax.broadcasted_iota(jnp.int32, (tq, n_cmp), 0)
    end_c = lax.broadcasted_iota(jnp.int32, (tq, n_cmp), 1) * CMP_STRIDE + (CMP_LEN - 1)
    bias_c = jnp.where(end_c <= t_c, 0.0, NEG)[None]
    seen = (t0 + lax.broadcasted_iota(jnp.int32, (tq, 1), 0)) >= CMP_LEN - 1
    kc = kc_ref[0, 0]
    vc = vc_ref[0, 0] + ones_ref[0:n_cmp, :]
    psum = jnp.zeros((tq, n_cmp), F32)
    acc_c = []
    for c in range(n_pairs):
        s_c = _dot_nt(q128[c * pair_rows:(c + 1) * pair_rows], kc).reshape(PAIR, tq, n_cmp) + bias_c
        e_c = jnp.exp2(s_c - jnp.max(s_c, axis=-1, keepdims=True))
        acc_c.append(_dot(e_c.reshape(pair_rows, n_cmp).astype(BF16), vc))
        inv_l = jnp.where(seen[None], 1.0 / jnp.sum(e_c, axis=-1, keepdims=True), 0.0)
        psum = psum + jnp.sum(e_c * inv_l, axis=0)

    ovt = ovt_ref[...]
    imp_t = sum(_dot_nt(ovt, piece) for piece in _split3(psum))
    jj = lax.broadcasted_iota(jnp.int32, (n_sel, tq), 0)
    tt = t0 + lax.broadcasted_iota(jnp.int32, (n_sel, tq), 1)
    cur = tt // SEL_LEN
    forced = (jj == 0) | (jj == cur) | (jj == cur - 1)
    score = jnp.where(forced, 1e6, jnp.where(jj * SEL_LEN <= tt, imp_t, -1e6))
    sub = lax.broadcasted_iota(jnp.int32, (SUBLANES, tq), 0)
    groups = [score[SUBLANES * v:SUBLANES * (v + 1), :] for v in range(n_sel // SUBLANES)]
    ranks = [jnp.zeros((SUBLANES, tq), F32) for _ in groups]
    for i in range(n_sel):
        vi, ri = divmod(i, SUBLANES)
        si = jnp.broadcast_to(groups[vi][ri:ri + 1, :], (SUBLANES, tq))
        for v, sv in enumerate(groups):
            if v > vi:
                before = jnp.where(si >= sv, 1.0, 0.0)
            elif v < vi:
                before = jnp.where(si > sv, 1.0, 0.0)
            else:
                before = jnp.where(sub > ri, jnp.where(si >= sv, 1.0, 0.0), jnp.where(si > sv, 1.0, 0.0))
            ranks[v] = ranks[v] + before
    pen_t = jnp.where(jnp.concatenate(ranks, axis=0) < float(topn), 0.0, NEG)
    pen_t = jnp.concatenate([pen_t, jnp.zeros((LANES - n_sel, tq), F32)], axis=0)
    pen = pen_t.T.astype(BF16)
    q256 = jnp.concatenate([q128, jnp.concatenate([pen] * hpg, axis=0)], axis=1)

    gate = jax.nn.sigmoid(ng_ref[0])
    low = lax.broadcasted_iota(jnp.int32, (tq, LANES), 1) < NSA_DV

    def gated(acc_pair, branch, c):
        a_e, a_o = acc_pair[:tq], acc_pair[tq:]
        keep = jnp.where(low, a_e, a_o)
        swap = pltpu.roll(jnp.where(low, a_o, a_e), NSA_DV, 1)
        col = branch * hpg + PAIR * c
        g2 = jnp.where(low, gate[:, col:col + 1], gate[:, col + 1:col + 2])
        return g2 * jnp.where(low, keep, swap) / jnp.where(low, swap, keep)

    w0 = pl.multiple_of(jnp.maximum(t0 - WINDOW, 0), LANES)
    kw = kw_ref[0, pl.ds(w0, WIN_KEYS), :] + posk_ref[pl.ds(w0, WIN_KEYS), :]
    vw = vw_ref[0, pl.ds(w0, WIN_KEYS), :] + ones_ref[...]
    pos_w = w0 + lax.broadcasted_iota(jnp.int32, (tq, WIN_KEYS), 1)
    t_w = t0 + lax.broadcasted_iota(jnp.int32, (tq, WIN_KEYS), 0)
    bias_w = jnp.where((pos_w <= t_w) & (pos_w > t_w - WINDOW), 0.0, NEG)[None]
    for c in range(n_pairs):
        s_w = _dot_nt(q128[c * pair_rows:(c + 1) * pair_rows], kw).reshape(PAIR, tq, WIN_KEYS) + bias_w
        e_w = jnp.exp2(s_w - jnp.max(s_w, axis=-1, keepdims=True))
        acc_w = _dot(e_w.reshape(pair_rows, WIN_KEYS).astype(BF16), vw)
        part_ref[:, c * LANES:(c + 1) * LANES] = jnp.where(seen, gated(acc_c[c], 0, c), 0.0) + gated(acc_w, 2, c)

    tk = SEL_KEY_TILE
    ones_v = ones_ref[0:tk, :]
    n_full = t0 // tk

    def keys(kt):
        k0 = pl.multiple_of(kt * tk, tk)
        return jnp.concatenate([ks_ref[0, pl.ds(k0, tk), :] + posk_ref[pl.ds(k0, tk), :],
                                oh_ref[pl.ds(k0, tk), :]], axis=1)

    def logits(kt):
        return _dot_nt(q256, keys(kt))

    def values(kt):
        return vs_ref[0, pl.ds(pl.multiple_of(kt * tk, tk), tk), :] + ones_v

    def row_max(s):
        return jnp.broadcast_to(jnp.max(s, axis=-1, keepdims=True), (s.shape[0], LANES))

    pos_d = n_full * tk + lax.broadcasted_iota(jnp.int32, (tq, tk), 1)
    t_d = t0 + lax.broadcasted_iota(jnp.int32, (tq, tk), 0)
    bias_d = jnp.where(pos_d <= t_d, 0.0, NEG)[None]
    k_d, v_d = keys(n_full), values(n_full)
    for c in range(n_pairs):
        r0, r1 = c * pair_rows, (c + 1) * pair_rows
        s_d = (_dot_nt(q256[r0:r1], k_d).reshape(PAIR, tq, tk) + bias_d).reshape(pair_rows, tk)
        m_d = row_max(s_d)
        m_ref[r0:r1] = m_d
        p_d = jnp.exp2(s_d - jnp.concatenate([m_d] * (tk // LANES), axis=1))
        acc_ref[r0:r1] = _dot(p_d.astype(BF16), v_d)

    k_0, k_1 = keys(0), keys(jnp.minimum(1, n_full))
    for c in range(n_pairs):
        r0, r1 = c * pair_rows, (c + 1) * pair_rows
        s_first = _dot_nt(q256[r0:r1], k_0)
        s_ref[0, r0:r1] = s_first
        mx_ref[0, r0:r1] = row_max(s_first)
        s_ref[1, r0:r1] = _dot_nt(q256[r0:r1], k_1)

    def body(kt, carry):
        s = s_ref[kt % 3]
        m_cur = mx_ref[kt % 2]
        mx_next = row_max(s_ref[(kt + 1) % 3])
        s_next = logits(jnp.minimum(kt + 2, n_full))
        m_prev = m_ref[...]
        m_new = jnp.maximum(m_prev, m_cur)
        p = jnp.exp2(s - jnp.concatenate([m_new] * (tk // LANES), axis=1))
        acc_ref[...] = acc_ref[...] * jnp.exp2(m_prev - m_new) + _dot(p.astype(BF16), values(kt))
        m_ref[...] = m_new
        mx_ref[(kt + 1) % 2] = mx_next
        s_ref[(kt + 2) % 3] = s_next
        return carry

    lax.fori_loop(0, n_full, body, 0)
    for c in range(n_pairs):
        sel = gated(acc_ref[c * pair_rows:(c + 1) * pair_rows], 1, c)
        o_ref[0, :, c * LANES:(c + 1) * LANES] = (part_ref[:, c * LANES:(c + 1) * LANES] + sel).astype(o_ref.dtype)


def _nsa(pa3, pf3, kcmp, vcmp):
    b, s, _ = pa3.shape
    g, hpg = NSA_GROUPS, NSA_HPG
    n_sel = s // SEL_LEN
    n_cmp = kcmp.shape[2]
    topn = min(SEL_TOPN, n_sel)
    rows = hpg * Q_TILE

    pos = jnp.arange(s, dtype=jnp.int32)
    ab = jnp.stack([pos // LANES] * 3 + [pos % LANES] * 3, axis=1).astype(F32)
    posk = jnp.zeros((s, LANES), F32).at[:, ALIBI_COL:ALIBI_COL + 6].set(ab).astype(BF16)
    onehot = (pos[:, None] // SEL_LEN == jnp.arange(LANES)[None, :]).astype(BF16)
    slopes = LOG2E * 2.0 ** (-8.0 * (jnp.arange(NSA_HEADS, dtype=F32) + 1.0) / NSA_HEADS)
    s1, s2, s3 = _split3(slopes)
    qcols = jnp.stack([s1.astype(F32) * LANES, s2.astype(F32) * LANES, s3.astype(F32) * LANES,
                       s1.astype(F32), s2.astype(F32), s3.astype(F32)], axis=1)
    qal = jnp.zeros((NSA_HEADS, LANES), F32).at[:, ALIBI_COL:ALIBI_COL + 6].set(qcols)
    qal = jnp.broadcast_to(qal.reshape(g, hpg, 1, LANES), (g, hpg, Q_TILE, LANES)).reshape(g, rows, LANES)
    qal = qal.astype(BF16)
    cstart = jnp.arange(n_cmp) * CMP_STRIDE
    sstart = jnp.arange(n_sel) * SEL_LEN
    ovt = ((cstart[None, :] < sstart[:, None] + SEL_LEN) & (cstart[None, :] + CMP_LEN > sstart[:, None])
           & (jnp.arange(n_cmp)[None, :] < n_cmp - 1)).astype(BF16)
    ones = jnp.zeros((WIN_KEYS, LANES), BF16).at[:, ONES_COL:].set(1.0)

    def seq_spec(col0):
        return pl.BlockSpec((1, s, LANES), lambda i, j, t: (i, 0, col0 // LANES + j))

    return pl.pallas_call(
        functools.partial(_nsa_kernel, topn=topn),
        grid=(b, g, s // Q_TILE),
        in_specs=[
            pl.BlockSpec((1, Q_TILE, hpg * LANES), lambda i, j, t: (i, t, PA_NQ // (hpg * LANES) + j)),
            pl.BlockSpec((1, Q_TILE, LANES), lambda i, j, t: (i, t, PF_NG // LANES + j)),
            pl.BlockSpec((1, 1, n_cmp, LANES), lambda i, j, t: (i, j, 0, 0)),
            pl.BlockSpec((1, 1, n_cmp, LANES), lambda i, j, t: (i, j, 0, 0)),
            seq_spec(PA_KS), seq_spec(PA_KW), seq_spec(PA_VS), seq_spec(PA_VW),
            _const_spec((s, LANES)), _const_spec((s, LANES)),
            pl.BlockSpec((1, rows, LANES), lambda i, j, t: (j, 0, 0)),
            _const_spec((n_sel, n_cmp)), _const_spec((WIN_KEYS, LANES)),
        ],
        out_specs=pl.BlockSpec((1, Q_TILE, hpg * NSA_DV), lambda i, j, t: (i, t, j)),
        out_shape=jax.ShapeDtypeStruct((b, s, NSA_HEADS * NSA_DV), BF16),
        scratch_shapes=[pltpu.VMEM((rows, LANES), F32), pltpu.VMEM((rows, LANES), F32),
                        pltpu.VMEM((3, rows, SEL_KEY_TILE), F32), pltpu.VMEM((2, rows, LANES), F32),
                        pltpu.VMEM((Q_TILE, hpg * NSA_DV), F32)],
        compiler_params=pltpu.CompilerParams(dimension_semantics=("parallel", "parallel", "parallel"),
                                             vmem_limit_bytes=VMEM_LIMIT),
        name="nsa_attention",
    )(pa3, pf3, kcmp, vcmp, pa3, pa3, pa3, pa3, posk, onehot, qal, ovt, ones)


def _post_kernel(x_ref, ret_ref, nsa_ref, p_ref, gpre_ref, gpost_ref, gfpre_ref, gfpost_ref, gn_ref,
                 wg_ref, wo_ref, wfi_ref, wfo_ref, wpg_ref, wpl_ref, o_ref, *, ff_chunk):
    d = D_MODEL
    x = x_ref[...]
    h = (_rms(x) * gpre_ref[...]).astype(BF16)
    ret = jax.nn.silu(_dot(h, wg_ref[:, 0:d])) * (ret_ref[...].astype(F32) * gn_ref[...])
    mixin = jax.nn.sigmoid(_dot(h, wg_ref[:, d:2 * d])) * ret
    mixin = mixin + jax.nn.sigmoid(_dot(h, wg_ref[:, 2 * d:3 * d])) * nsa_ref[...].astype(F32)
    x1 = x + _rms(_dot(mixin.astype(BF16), wo_ref[...])) * gpost_ref[...]

    h2 = (_rms(x1) * gfpre_ref[...]).astype(BF16)
    f = jnp.zeros(x.shape, F32)
    for c0 in range(0, D_FF, ff_chunk):
        u = _dot(h2, wfi_ref[:, c0:c0 + ff_chunk])
        v = _dot(h2, wfi_ref[:, D_FF + c0:D_FF + c0 + ff_chunk])
        f = f + _dot((jax.nn.silu(u) * v).astype(BF16), wfo_ref[c0:c0 + ff_chunk, :])
    x2 = x1 + _rms(f) * gfpost_ref[...]

    gate = jax.nn.sigmoid(_dot(_rms(x2).astype(BF16), wpg_ref[...]))
    o_ref[...] = x2 + gate * _dot(p_ref[...].astype(BF16), wpl_ref[...])


def _post(x2, ret2, nsa2, p2, gpre, gpost, gfpre, gfpost, gn, wg, wo, wfi, wfo, wpg, wpl, tm, ff_chunk):
    t, d = x2.shape

    def row_spec(w):
        return pl.BlockSpec((tm, w), lambda i: (i, 0))

    vec = _const_spec((1, d))
    return pl.pallas_call(
        functools.partial(_post_kernel, ff_chunk=ff_chunk),
        grid=(t // tm,),
        in_specs=[row_spec(d), row_spec(d), row_spec(d), row_spec(PLE_DIM), vec, vec, vec, vec, vec,
                  _const_spec(wg.shape), _const_spec(wo.shape), _const_spec(wfi.shape), _const_spec(wfo.shape),
                  _const_spec(wpg.shape), _const_spec(wpl.shape)],
        out_specs=row_spec(d),
        out_shape=jax.ShapeDtypeStruct((t, d), F32),
        compiler_params=pltpu.CompilerParams(dimension_semantics=("parallel",), vmem_limit_bytes=VMEM_LIMIT),
        name="post",
    )(x2, ret2, nsa2, p2, gpre, gpost, gfpre, gfpost, gn, wg, wo, wfi, wfo, wpg, wpl)


def _pad_slabs(w, n, width):
    d = w.shape[0]
    return jnp.pad(w.reshape(d, n, width), ((0, 0), (0, 0), (0, LANES - width))).reshape(d, n * LANES)


def _layer(x, p, g_mix_pre, g_mix_post, g_ffn_pre, g_ffn_post, w_in, gn_g, pe_k, w1_k, w2_k, pe_v, w1_v, w2_v,
           w_out, w_ffn_in, w_ffn_out, w_ple, w_ple_gate):
    b, s, d = x.shape
    t = b * s
    g, hpg = NSA_GROUPS, NSA_HPG
    offs = [0]
    for n in IN_SPLITS:
        offs.append(offs[-1] + n)
    (w_rq, w_rk, w_rv, w_rg, w_nq, w_kc, w_vc, w_ks, w_vs, w_kw, w_vw, w_ng, w_mg) = [
        w_in[:, offs[i]:offs[i + 1]] for i in range(len(IN_SPLITS))]

    w_a = jnp.concatenate([
        w_rq, w_rk * (RET_DK ** -0.5), w_rv,
        _pad_slabs(w_nq * (LOG2E * NSA_DK ** -0.5), NSA_HEADS, NSA_DK),
        _pad_slabs(w_ks, g, NSA_DK), _pad_slabs(w_kw, g, NSA_DK),
        _pad_slabs(w_vs, g, NSA_DV), _pad_slabs(w_vw, g, NSA_DV)], axis=1).astype(BF16)
    w_ng_g = w_ng.reshape(d, 3, g, hpg).transpose(0, 2, 1, 3).reshape(d, g, 3 * hpg)
    w_f = jnp.concatenate([
        _pad_slabs(w_kc, g, NSA_DK), w_vc,
        jnp.pad(w_ng_g, ((0, 0), (0, 0), (0, LANES - 3 * hpg))).reshape(d, g * LANES)], axis=1).astype(BF16)

    x2 = x.reshape(t, d)
    gpre = g_mix_pre.reshape(1, d)
    pa, pf = _proj(x2, gpre, w_a, w_f, 512, 1280)
    pa3 = pa.reshape(b, s, PA_TOTAL)
    pf3 = pf.reshape(b, s, PF_TOTAL)

    ret = _retention(pa3)

    nh = s // CMP_STRIDE
    cend = jnp.arange(nh, dtype=jnp.int32) * CMP_STRIDE + (CMP_LEN - 1)
    ab = jnp.stack([cend // LANES] * 3 + [cend % LANES] * 3, axis=1).astype(F32)
    posc = jnp.zeros((nh, LANES), F32).at[:, ALIBI_COL:ALIBI_COL + 6].set(ab)
    kcmp = _compress(pf3, PF_KC // LANES, 1, pe_k, w1_k, w2_k, posc)
    vcmp = _compress(pf3, PF_VC // LANES, 0, pe_v, w1_v, w2_v, jnp.zeros((nh, LANES), F32))

    nsa = _nsa(pa3, pf3, kcmp, vcmp)

    w_g = jnp.concatenate([w_rg, w_mg], axis=1).astype(BF16)
    out = _post(x2, ret.reshape(t, d), nsa.reshape(t, d), p.reshape(t, PLE_DIM),
                gpre, g_mix_post.reshape(1, d), g_ffn_pre.reshape(1, d), g_ffn_post.reshape(1, d),
                gn_g.reshape(1, d), w_g, w_out.astype(BF16), w_ffn_in.astype(BF16), w_ffn_out.astype(BF16),
                w_ple_gate.astype(BF16), w_ple.astype(BF16), 512, 256)
    return out.reshape(b, s, d)


def kernel(x, p, norm_mix_pre, norm_mix_post, norm_ffn_pre, norm_ffn_post, w_in, ret_gn_g, cmp_pe_k, cmp_w1_k,
           cmp_w2_k, cmp_pe_v, cmp_w1_v, cmp_w2_v, w_out, w_ffn_in, w_ffn_out, w_ple, w_ple_gate):
    for i in range(w_in.shape[0]):
        x = _layer(x, p[i], norm_mix_pre[i], norm_mix_post[i], norm_ffn_pre[i], norm_ffn_post[i], w_in[i],
                   ret_gn_g[i], cmp_pe_k[i], cmp_w1_k[i], cmp_w2_k[i], cmp_pe_v[i], cmp_w1_v[i], cmp_w2_v[i],
                   w_out[i], w_ffn_in[i], w_ffn_out[i], w_ple[i], w_ple_gate[i])
    return x
```

```python
import functools

import jax
import jax.numpy as jnp
from jax import lax
from jax.experimental import pallas as pl
from jax.experimental.pallas import tpu as pltpu

F32 = jnp.float32
BF16 = jnp.bfloat16

D_MODEL = 1024
PLE_DIM = 256
EPS = 1e-6
GN_EPS = 1e-5
NEG = -1e30
LOG2E = 1.4426950408889634

RET_HEADS = 8
RET_DV = D_MODEL // RET_HEADS
RET_DK = RET_DV // 2
RET_CHUNK = 128
RET_BATCH = 4

NSA_HEADS = 16
NSA_GROUPS = 2
NSA_HPG = NSA_HEADS // NSA_GROUPS
NSA_DV = D_MODEL // NSA_HEADS
NSA_DK = 3 * NSA_DV // 2
CMP_LEN = 32
CMP_STRIDE = 16
SEL_LEN = 64
SEL_TOPN = 16
WINDOW = 512
D_FF = -(-8 * D_MODEL // (3 * 256)) * 256

IN_SPLITS = (
    RET_HEADS * RET_DK, RET_HEADS * RET_DK, RET_HEADS * RET_DV, RET_HEADS * RET_DV,
    NSA_HEADS * NSA_DK, NSA_GROUPS * NSA_DK, NSA_GROUPS * NSA_DV, NSA_GROUPS * NSA_DK,
    NSA_GROUPS * NSA_DV, NSA_GROUPS * NSA_DK, NSA_GROUPS * NSA_DV, 3 * NSA_HEADS, 2 * D_MODEL,
)

LANES = 128
Q_TILE = 256
SEL_KEY_TILE = 256
WIN_KEYS = WINDOW + Q_TILE
ALIBI_COL = NSA_DK
BOUND_COL = ALIBI_COL + 6
DENOM_FLOOR = 2.0 ** -100
ONES_COL = NSA_DV
SUBLANES = 8
PAIR = LANES // NSA_DV
VMEM_LIMIT = 56 * 1024 * 1024

PA_RQ, PA_RK, PA_RV = 0, 512, 1024
PA_NQ = 2048
PA_KS = PA_NQ + NSA_HEADS * LANES
PA_KW = PA_KS + NSA_GROUPS * LANES
PA_VS = PA_KW + NSA_GROUPS * LANES
PA_VW = PA_VS + NSA_GROUPS * LANES
PA_TOTAL = PA_VW + NSA_GROUPS * LANES
PF_KC = 0
PF_VC = PF_KC + NSA_GROUPS * LANES
PF_NG = PF_VC + LANES
PF_TOTAL = PF_NG + NSA_GROUPS * LANES


def _dot(a, b):
    return jnp.dot(a, b, preferred_element_type=F32)


def _dot_nt(a, b):
    return lax.dot_general(a, b, (((1,), (1,)), ((), ())), preferred_element_type=F32)


def _dot_tn(a, b):
    return lax.dot_general(a, b, (((0,), (0,)), ((), ())), preferred_element_type=F32)


def _rms(x):
    return x * lax.rsqrt(jnp.mean(x * x, axis=-1, keepdims=True) + EPS)


def _const_spec(shape):
    nd = len(shape)
    return pl.BlockSpec(shape, lambda *_: (0,) * nd, pipeline_mode=pl.Buffered(1))


def _proj_kernel(x_ref, g_ref, wa_ref, wf_ref, oa_ref, of_ref, *, chunk):
    h = (_rms(x_ref[...]) * g_ref[...]).astype(BF16)
    for w_ref, o_ref in ((wa_ref, oa_ref), (wf_ref, of_ref)):
        n = o_ref.shape[-1]
        for c0 in range(0, n, chunk):
            c1 = min(c0 + chunk, n)
            o_ref[:, c0:c1] = _dot(h, w_ref[:, c0:c1]).astype(o_ref.dtype)


def _proj(x2, g, w_a, w_f, tm, chunk):
    t, d = x2.shape
    na, nf = w_a.shape[1], w_f.shape[1]
    return pl.pallas_call(
        functools.partial(_proj_kernel, chunk=chunk),
        grid=(t // tm,),
        in_specs=[pl.BlockSpec((tm, d), lambda i: (i, 0)), _const_spec((1, d)), _const_spec((d, na)),
                  _const_spec((d, nf))],
        out_specs=[pl.BlockSpec((tm, na), lambda i: (i, 0)), pl.BlockSpec((tm, nf), lambda i: (i, 0))],
        out_shape=[jax.ShapeDtypeStruct((t, na), BF16), jax.ShapeDtypeStruct((t, nf), F32)],
        compiler_params=pltpu.CompilerParams(dimension_semantics=("parallel",), vmem_limit_bytes=VMEM_LIMIT),
        name="proj",
    )(x2, g, w_a, w_f)


def _ret_kernel(q_ref, k_ref, v_ref, dmat_ref, xi_ref, zeta_ref, dec_ref, o_ref, state_ref):
    @pl.when(pl.program_id(1) == 0)
    def _():
        state_ref[...] = jnp.zeros_like(state_ref)

    nb = q_ref.shape[0]
    hs = [(i, h) for i in range(nb) for h in range(RET_HEADS)]
    q = [q_ref[i, :, h * RET_DK:(h + 1) * RET_DK] for i, h in hs]
    k = [k_ref[i, :, h * RET_DK:(h + 1) * RET_DK] for i, h in hs]
    v = [v_ref[i, :, h * RET_DV:(h + 1) * RET_DV] for i, h in hs]
    r = [state_ref[i, h] for i, h in hs]
    n = range(len(hs))
    inner = [_dot_nt(q[j], k[j]) for j in n]
    cross = [_dot(q[j], r[j].astype(BF16)) for j in n]
    kz = [(k[j].astype(F32) * zeta_ref[hs[j][1]]).astype(BF16) for j in n]
    upd = [_dot_tn(kz[j], v[j]) for j in n]
    inner = [(inner[j] * dmat_ref[hs[j][1]]).astype(BF16) for j in n]
    o = [_dot(inner[j], v[j]) + cross[j] * xi_ref[hs[j][1]] for j in n]
    for j, (i, h) in enumerate(hs):
        state_ref[i, h] = r[j] * dec_ref[h] + upd[j]
    mu = [jnp.mean(o[j], axis=-1, keepdims=True) for j in n]
    d = [o[j] - mu[j] for j in n]
    var = [jnp.mean(d[j] * d[j], axis=-1, keepdims=True) for j in n]
    for j, (i, h) in enumerate(hs):
        o_ref[i, :, h * RET_DV:(h + 1) * RET_DV] = (d[j] * lax.rsqrt(var[j] + GN_EPS)).astype(o_ref.dtype)


def _retention(pa3):
    b, s, _ = pa3.shape
    c = RET_CHUNK
    hh = RET_HEADS
    log_g = jnp.log1p(-(2.0 ** (-5.0 - jnp.arange(hh, dtype=F32))))
    idx = jnp.arange(c, dtype=F32)
    diff = idx[:, None] - idx[None, :]
    dmat = jnp.exp(log_g[:, None, None] * jnp.maximum(diff, 0.0)) * (diff >= 0)
    xi = jnp.broadcast_to(jnp.exp(log_g[:, None] * (idx + 1.0))[:, :, None], (hh, c, RET_DV))
    zeta = jnp.broadcast_to(jnp.exp(log_g[:, None] * (c - 1.0 - idx))[:, :, None], (hh, c, RET_DK))
    dec = jnp.broadcast_to(jnp.exp(log_g * c)[:, None, None], (hh, RET_DK, RET_DV))
    wq = hh * RET_DK
    wv = hh * RET_DV
    nb = RET_BATCH if b % RET_BATCH == 0 else 1
    return pl.pallas_call(
        _ret_kernel,
        grid=(b // nb, s // c),
        in_specs=[
            pl.BlockSpec((nb, c, wq), lambda i, j: (i, j, PA_RQ // wq)),
            pl.BlockSpec((nb, c, wq), lambda i, j: (i, j, PA_RK // wq)),
            pl.BlockSpec((nb, c, wv), lambda i, j: (i, j, PA_RV // wv)),
            _const_spec((hh, c, c)), _const_spec((hh, c, RET_DV)), _const_spec((hh, c, RET_DK)),
            _const_spec((hh, RET_DK, RET_DV)),
        ],
        out_specs=pl.BlockSpec((nb, c, wv), lambda i, j: (i, j, 0)),
        out_shape=jax.ShapeDtypeStruct((b, s, wv), BF16),
        scratch_shapes=[pltpu.VMEM((nb, hh, RET_DK, RET_DV), F32)],
        compiler_params=pltpu.CompilerParams(dimension_semantics=("parallel", "arbitrary")),
        name="retention",
    )(pa3, pa3, pa3, dmat, xi, zeta, dec)


def _cmp_kernel(src_ref, pe_ref, w1_ref, w2_ref, tab_ref, o_ref):
    n = o_ref.shape[2]
    top = jnp.zeros((n, LANES), F32)
    bot = jnp.zeros((n, LANES), F32)
    for l in range(CMP_STRIDE):
        x = src_ref[0, pl.ds(l, n, stride=CMP_STRIDE), :]
        top = top + _dot((x + pe_ref[0, l:l + 1, :]).astype(BF16), w1_ref[0, l])
        bot = bot + _dot((x + pe_ref[0, CMP_STRIDE + l:CMP_STRIDE + l + 1, :]).astype(BF16),
                         w1_ref[0, CMP_STRIDE + l])
    pre = top + pltpu.roll(bot, n - 1, 0)
    act = jax.nn.gelu(pre)
    o_ref[0, 0] = (_dot(act.astype(BF16), w2_ref[...]) + tab_ref[...]).astype(o_ref.dtype)


def _compress(pf3, slab0, slab_step, pe, w1, w2, tab):
    b, s, _ = pf3.shape
    g = NSA_GROUPS
    n = s // CMP_STRIDE
    d = w2.shape[0]
    lane_step = 0 if slab_step else d
    pe_g = jnp.stack([jnp.pad(pe, ((0, 0), (j * lane_step, LANES - d - j * lane_step))) for j in range(g)])
    w1_g = jnp.stack([jnp.pad(w1.reshape(CMP_LEN, d, d), ((0, 0), (j * lane_step, LANES - d - j * lane_step),
                                                            (0, LANES - d))) for j in range(g)]).astype(BF16)
    w2p = jnp.pad(w2, ((0, LANES - d), (0, LANES - d))).astype(BF16)
    return pl.pallas_call(
        _cmp_kernel,
        grid=(b, g),
        in_specs=[pl.BlockSpec((1, s, LANES), lambda i, j: (i, 0, slab0 + j * slab_step)),
                  pl.BlockSpec((1, CMP_LEN, LANES), lambda i, j: (j, 0, 0)),
                  pl.BlockSpec((1, CMP_LEN, LANES, LANES), lambda i, j: (j, 0, 0, 0)),
                  _const_spec((LANES, LANES)), _const_spec((n, LANES))],
        out_specs=pl.BlockSpec((1, 1, n, LANES), lambda i, j: (i, j, 0, 0)),
        out_shape=jax.ShapeDtypeStruct((b, g, n, LANES), BF16),
        compiler_params=pltpu.CompilerParams(dimension_semantics=("parallel", "parallel")),
        name="compress_%d" % d,
    )(pf3, pe_g, w1_g, w2p, tab)


def _split3(x):
    p1 = x.astype(BF16)
    r1 = x - p1.astype(F32)
    p2 = r1.astype(BF16)
    p3 = (r1 - p2.astype(F32)).astype(BF16)
    return p1, p2, p3


def _nsa_body(q_ref, ng_ref, kc_ref, vc_ref, ks_ref, kw_ref, vs_ref, vw_ref, posk_ref, oh_ref, qal_ref,
              slope_ref, ovt_ref, ones_ref, o_ref, m_ref, acc_ref, s_ref, mx_ref, part_ref, cmax_ref, *, topn, fast):
    hpg = NSA_HPG
    tq = Q_TILE
    t0 = pl.program_id(2) * tq
    n_sel = ovt_ref.shape[0]
    n_cmp = kc_ref.shape[2]
    pair_rows = PAIR * tq
    n_pairs = hpg // PAIR
    bounds = [(c * pair_rows, (c + 1) * pair_rows) for c in range(n_pairs)]
    lane_q = lax.broadcasted_iota(jnp.int32, (tq, LANES), 1)

    q_all = q_ref[0]
    q_heads = [q_all[:, h * LANES:(h + 1) * LANES] for h in range(hpg)]
    if fast:
        t_abs = (t0 + lax.broadcasted_iota(jnp.int32, (tq, LANES), 0)).astype(F32)
        cmax = cmax_ref[...]
        for h in range(hpg):
            m_up = _dot(jnp.abs(q_heads[h]), cmax) + slope_ref[0, h * tq:(h + 1) * tq, :] * t_abs + 1.0
            m1 = m_up.astype(BF16)
            m2 = (m_up - m1.astype(F32)).astype(BF16)
            q_heads[h] = jnp.where(lane_q == BOUND_COL, -m1, jnp.where(lane_q == BOUND_COL + 1, -m2, q_heads[h]))
    q128 = jnp.concatenate(q_heads, axis=0) + qal_ref[0]

    def softmax_num(s):
        return jnp.exp2(s) if fast else jnp.exp2(s - jnp.max(s, axis=-1, keepdims=True))

    t_c = t0 + lax.broadcasted_iota(jnp.int32, (tq, n_cmp), 0)
    end_c = lax.broadcasted_iota(jnp.int32, (tq, n_cmp), 1) * CMP_STRIDE + (CMP_LEN - 1)
    bias_c = jnp.where(end_c <= t_c, 0.0, NEG)[None]
    seen = (t0 + lax.broadcasted_iota(jnp.int32, (tq, 1), 0)) >= CMP_LEN - 1
    kc = kc_ref[0, 0]
    vc = vc_ref[0, 0] + ones_ref[0:n_cmp, :]
    psum = jnp.zeros((tq, n_cmp), F32)
    l_min = jnp.full((PAIR, tq, 1), 1.0, F32)
    acc_c = []
    for r0, r1 in bounds:
        e_c = softmax_num(_dot_nt(q128[r0:r1], kc).reshape(PAIR, tq, n_cmp) + bias_c)
        acc_c.append(_dot(e_c.reshape(pair_rows, n_cmp).astype(BF16), vc))
        l_c = jnp.sum(e_c, axis=-1, keepdims=True)
        l_min = jnp.minimum(l_min, jnp.where(seen[None], l_c, 1.0))
        psum = psum + jnp.sum(e_c * jnp.where(seen[None], 1.0 / l_c, 0.0), axis=0)

    ovt = ovt_ref[...]
    imp_t = sum(_dot_nt(ovt, piece) for piece in _split3(psum))
    jj = lax.broadcasted_iota(jnp.int32, (n_sel, tq), 0)
    tt = t0 + lax.broadcasted_iota(jnp.int32, (n_sel, tq), 1)
    cur = tt // SEL_LEN
    forced = (jj == 0) | (jj == cur) | (jj == cur - 1)
    score = jnp.where(forced, 1e6, jnp.where(jj * SEL_LEN <= tt, imp_t, -1e6))
    sub = lax.broadcasted_iota(jnp.int32, (SUBLANES, tq), 0)
    groups = [score[SUBLANES * v:SUBLANES * (v + 1), :] for v in range(n_sel // SUBLANES)]
    ranks = [jnp.zeros((SUBLANES, tq), F32) for _ in groups]
    for i in range(n_sel):
        vi, ri = divmod(i, SUBLANES)
        si = jnp.broadcast_to(groups[vi][ri:ri + 1, :], (SUBLANES, tq))
        for v, sv in enumerate(groups):
            if v > vi:
                before = jnp.where(si >= sv, 1.0, 0.0)
            elif v < vi:
                before = jnp.where(si > sv, 1.0, 0.0)
            else:
                before = jnp.where(sub > ri, jnp.where(si >= sv, 1.0, 0.0), jnp.where(si > sv, 1.0, 0.0))
            ranks[v] = ranks[v] + before
    pen_t = jnp.where(jnp.concatenate(ranks, axis=0) < float(topn), 0.0, NEG)
    pen_t = jnp.concatenate([pen_t, jnp.zeros((LANES - n_sel, tq), F32)], axis=0)
    pen = pen_t.T.astype(BF16)
    q256 = jnp.concatenate([q128, jnp.concatenate([pen] * hpg, axis=0)], axis=1)

    gate = jax.nn.sigmoid(ng_ref[0])
    low = lane_q < NSA_DV
    l_acc = [jnp.full((pair_rows, LANES), 1.0, F32)]

    def gated(acc_pair, branch, c):
        if branch:
            l_acc[0] = jnp.minimum(l_acc[0], jnp.where(jnp.concatenate([low] * PAIR, axis=0), 1.0, acc_pair))
        a_e, a_o = acc_pair[:tq], acc_pair[tq:]
        keep = jnp.where(low, a_e, a_o)
        swap = pltpu.roll(jnp.where(low, a_o, a_e), NSA_DV, 1)
        col = branch * hpg + PAIR * c
        g2 = jnp.where(low, gate[:, col:col + 1], gate[:, col + 1:col + 2])
        return g2 * jnp.where(low, keep, swap) / jnp.where(low, swap, keep)

    w0 = pl.multiple_of(jnp.maximum(t0 - WINDOW, 0), LANES)
    kw = kw_ref[0, pl.ds(w0, WIN_KEYS), :] + posk_ref[pl.ds(w0, WIN_KEYS), :]
    vw = vw_ref[0, pl.ds(w0, WIN_KEYS), :] + ones_ref[...]
    pos_w = w0 + lax.broadcasted_iota(jnp.int32, (tq, WIN_KEYS), 1)
    t_w = t0 + lax.broadcasted_iota(jnp.int32, (tq, WIN_KEYS), 0)
    bias_w = jnp.where((pos_w <= t_w) & (pos_w > t_w - WINDOW), 0.0, NEG)[None]
    for c, (r0, r1) in enumerate(bounds):
        e_w = softmax_num(_dot_nt(q128[r0:r1], kw).reshape(PAIR, tq, WIN_KEYS) + bias_w)
        acc_w = _dot(e_w.reshape(pair_rows, WIN_KEYS).astype(BF16), vw)
        part_ref[:, c * LANES:(c + 1) * LANES] = jnp.where(seen, gated(acc_c[c], 0, c), 0.0) + gated(acc_w, 2, c)

    tk = SEL_KEY_TILE
    ones_v = ones_ref[0:tk, :]
    n_full = t0 // tk

    def keys(kt):
        k0 = pl.multiple_of(kt * tk, tk)
        return jnp.concatenate([ks_ref[0, pl.ds(k0, tk), :] + posk_ref[pl.ds(k0, tk), :],
                                oh_ref[pl.ds(k0, tk), :]], axis=1)

    def logits(kt):
        return _dot_nt(q256, keys(kt))

    def values(kt):
        return vs_ref[0, pl.ds(pl.multiple_of(kt * tk, tk), tk), :] + ones_v

    def row_max(s):
        return jnp.broadcast_to(jnp.max(s, axis=-1, keepdims=True), (s.shape[0], LANES))

    pos_d = n_full * tk + lax.broadcasted_iota(jnp.int32, (tq, tk), 1)
    t_d = t0 + lax.broadcasted_iota(jnp.int32, (tq, tk), 0)
    bias_d = jnp.where(pos_d <= t_d, 0.0, NEG)[None]
    k_d, v_d = keys(n_full), values(n_full)
    for r0, r1 in bounds:
        s_d = (_dot_nt(q256[r0:r1], k_d).reshape(PAIR, tq, tk) + bias_d).reshape(pair_rows, tk)
        if fast:
            p_d = jnp.exp2(s_d)
        else:
            m_d = row_max(s_d)
            m_ref[r0:r1] = m_d
            p_d = jnp.exp2(s_d - jnp.concatenate([m_d] * (tk // LANES), axis=1))
        acc_ref[r0:r1] = _dot(p_d.astype(BF16), v_d)

    if fast:
        s_ref[0] = logits(0)

        def body(kt, carry):
            s = s_ref[0]
            s_next = logits(jnp.minimum(kt + 1, n_full))
            acc_ref[...] = acc_ref[...] + _dot(jnp.exp2(s).astype(BF16), values(kt))
            s_ref[0] = s_next
            return carry
    else:
        k_0, k_1 = keys(0), keys(jnp.minimum(1, n_full))
        for r0, r1 in bounds:
            s_first = _dot_nt(q256[r0:r1], k_0)
            s_ref[0, r0:r1] = s_first
            mx_ref[0, r0:r1] = row_max(s_first)
            s_ref[1, r0:r1] = _dot_nt(q256[r0:r1], k_1)

        def body(kt, carry):
            s = s_ref[kt % 3]
            m_cur = mx_ref[kt % 2]
            mx_next = row_max(s_ref[(kt + 1) % 3])
            s_next = logits(jnp.minimum(kt + 2, n_full))
            m_prev = m_ref[...]
            m_new = jnp.maximum(m_prev, m_cur)
            p = jnp.exp2(s - jnp.concatenate([m_new] * (tk // LANES), axis=1))
            acc_ref[...] = acc_ref[...] * jnp.exp2(m_prev - m_new) + _dot(p.astype(BF16), values(kt))
            m_ref[...] = m_new
            mx_ref[(kt + 1) % 2] = mx_next
            s_ref[(kt + 2) % 3] = s_next
            return carry

    lax.fori_loop(0, n_full, body, 0)
    for c, (r0, r1) in enumerate(bounds):
        sel = gated(acc_ref[r0:r1], 1, c)
        o_ref[0, :, c * LANES:(c + 1) * LANES] = (part_ref[:, c * LANES:(c + 1) * LANES] + sel).astype(o_ref.dtype)
    return jnp.minimum(jnp.min(l_min), jnp.min(l_acc[0]))


def _nsa_kernel(q_ref, ng_ref, kc_ref, vc_ref, ks_ref, kw_ref, vs_ref, vw_ref, posk_ref, oh_ref, qal_ref, slope_ref,
                ovt_ref, ones_ref, o_ref, m_ref, acc_ref, s_ref, mx_ref, part_ref, cmax_ref, *, topn):
    refs = (q_ref, ng_ref, kc_ref, vc_ref, ks_ref, kw_ref, vs_ref, vw_ref, posk_ref, oh_ref, qal_ref, slope_ref,
            ovt_ref, ones_ref, o_ref, m_ref, acc_ref, s_ref, mx_ref, part_ref, cmax_ref)

    @pl.when(pl.program_id(2) == 0)
    def _():
        def col_max(k):
            return jnp.max(jnp.abs(k.astype(F32)), axis=0, keepdims=True)

        cm = jnp.maximum(jnp.maximum(col_max(ks_ref[0]), col_max(kw_ref[0])), col_max(kc_ref[0, 0]))
        feat = lax.broadcasted_iota(jnp.int32, (1, LANES), 1) < NSA_DK
        cmax_ref[...] = jnp.broadcast_to(jnp.where(feat, cm, 0.0), (LANES, LANES)).T.astype(BF16)

    l_min = _nsa_body(*refs, topn=topn, fast=True)

    @pl.when(jnp.logical_not(l_min > DENOM_FLOOR))
    def _():
        _nsa_body(*refs, topn=topn, fast=False)


def _nsa(pa3, pf3, kcmp, vcmp):
    b, s, _ = pa3.shape
    g, hpg = NSA_GROUPS, NSA_HPG
    n_sel = s // SEL_LEN
    n_cmp = kcmp.shape[2]
    topn = min(SEL_TOPN, n_sel)
    rows = hpg * Q_TILE

    pos = jnp.arange(s, dtype=jnp.int32)
    ab = jnp.stack([pos // LANES] * 3 + [pos % LANES] * 3, axis=1).astype(F32)
    posk = jnp.zeros((s, LANES), F32).at[:, ALIBI_COL:ALIBI_COL + 6].set(ab)
    posk = posk.at[:, BOUND_COL:BOUND_COL + 2].set(1.0).astype(BF16)
    onehot = (pos[:, None] // SEL_LEN == jnp.arange(LANES)[None, :]).astype(BF16)
    slopes = LOG2E * 2.0 ** (-8.0 * (jnp.arange(NSA_HEADS, dtype=F32) + 1.0) / NSA_HEADS)
    s1, s2, s3 = _split3(slopes)
    qcols = jnp.stack([s1.astype(F32) * LANES, s2.astype(F32) * LANES, s3.astype(F32) * LANES,
                       s1.astype(F32), s2.astype(F32), s3.astype(F32)], axis=1)
    qal = jnp.zeros((NSA_HEADS, LANES), F32).at[:, ALIBI_COL:ALIBI_COL + 6].set(qcols)
    qal = jnp.broadcast_to(qal.reshape(g, hpg, 1, LANES), (g, hpg, Q_TILE, LANES)).reshape(g, rows, LANES)
    qal = qal.astype(BF16)
    slope_rows = jnp.broadcast_to(slopes.reshape(g, hpg, 1, 1), (g, hpg, Q_TILE, LANES)).reshape(g, rows, LANES)
    cstart = jnp.arange(n_cmp) * CMP_STRIDE
    sstart = jnp.arange(n_sel) * SEL_LEN
    ovt = ((cstart[None, :] < sstart[:, None] + SEL_LEN) & (cstart[None, :] + CMP_LEN > sstart[:, None])
           & (jnp.arange(n_cmp)[None, :] < n_cmp - 1)).astype(BF16)
    ones = jnp.zeros((WIN_KEYS, LANES), BF16).at[:, ONES_COL:].set(1.0)

    def seq_spec(col0):
        return pl.BlockSpec((1, s, LANES), lambda i, j, t: (i, 0, col0 // LANES + j))

    return pl.pallas_call(
        functools.partial(_nsa_kernel, topn=topn),
        grid=(b, g, s // Q_TILE),
        in_specs=[
            pl.BlockSpec((1, Q_TILE, hpg * LANES), lambda i, j, t: (i, t, PA_NQ // (hpg * LANES) + j)),
            pl.BlockSpec((1, Q_TILE, LANES), lambda i, j, t: (i, t, PF_NG // LANES + j)),
            pl.BlockSpec((1, 1, n_cmp, LANES), lambda i, j, t: (i, j, 0, 0)),
            pl.BlockSpec((1, 1, n_cmp, LANES), lambda i, j, t: (i, j, 0, 0)),
            seq_spec(PA_KS), seq_spec(PA_KW), seq_spec(PA_VS), seq_spec(PA_VW),
            _const_spec((s, LANES)), _const_spec((s, LANES)),
            pl.BlockSpec((1, rows, LANES), lambda i, j, t: (j, 0, 0)),
            pl.BlockSpec((1, rows, LANES), lambda i, j, t: (j, 0, 0)),
            _const_spec((n_sel, n_cmp)), _const_spec((WIN_KEYS, LANES)),
        ],
        out_specs=pl.BlockSpec((1, Q_TILE, hpg * NSA_DV), lambda i, j, t: (i, t, j)),
        out_shape=jax.ShapeDtypeStruct((b, s, NSA_HEADS * NSA_DV), BF16),
        scratch_shapes=[pltpu.VMEM((rows, LANES), F32), pltpu.VMEM((rows, LANES), F32),
                        pltpu.VMEM((3, rows, SEL_KEY_TILE), F32), pltpu.VMEM((2, rows, LANES), F32),
                        pltpu.VMEM((Q_TILE, hpg * NSA_DV), F32), pltpu.VMEM((LANES, LANES), BF16)],
        compiler_params=pltpu.CompilerParams(dimension_semantics=("parallel", "parallel", "arbitrary"),
                                             vmem_limit_bytes=VMEM_LIMIT),
        name="nsa_attention",
    )(pa3, pf3, kcmp, vcmp, pa3, pa3, pa3, pa3, posk, onehot, qal, slope_rows, ovt, ones)


def _post_kernel(x_ref, ret_ref, nsa_ref, p_ref, gpre_ref, gpost_ref, gfpre_ref, gfpost_ref, gn_ref,
                 wg_ref, wo_ref, wfi_ref, wfo_ref, wpg_ref, wpl_ref, o_ref, *, ff_chunk):
    d = D_MODEL
    x = x_ref[...]
    h = (_rms(x) * gpre_ref[...]).astype(BF16)
    ret = jax.nn.silu(_dot(h, wg_ref[:, 0:d])) * (ret_ref[...].astype(F32) * gn_ref[...])
    mixin = jax.nn.sigmoid(_dot(h, wg_ref[:, d:2 * d])) * ret
    mixin = mixin + jax.nn.sigmoid(_dot(h, wg_ref[:, 2 * d:3 * d])) * nsa_ref[...].astype(F32)
    x1 = x + _rms(_dot(mixin.astype(BF16), wo_ref[...])) * gpost_ref[...]

    h2 = (_rms(x1) * gfpre_ref[...]).astype(BF16)
    f = jnp.zeros(x.shape, F32)
    for c0 in range(0, D_FF, ff_chunk):
        u = _dot(h2, wfi_ref[:, c0:c0 + ff_chunk])
        v = _dot(h2, wfi_ref[:, D_FF + c0:D_FF + c0 + ff_chunk])
        f = f + _dot((jax.nn.silu(u) * v).astype(BF16), wfo_ref[c0:c0 + ff_chunk, :])
    x2 = x1 + _rms(f) * gfpost_ref[...]

    gate = jax.nn.sigmoid(_dot(_rms(x2).astype(BF16), wpg_ref[...]))
    o_ref[...] = x2 + gate * _dot(p_ref[...].astype(BF16), wpl_ref[...])


def _post(x2, ret2, nsa2, p2, gpre, gpost, gfpre, gfpost, gn, wg, wo, wfi, wfo, wpg, wpl, tm, ff_chunk):
    t, d = x2.shape

    def row_spec(w):
        return pl.BlockSpec((tm, w), lambda i: (i, 0))

    vec = _const_spec((1, d))
    return pl.pallas_call(
        functools.partial(_post_kernel, ff_chunk=ff_chunk),
        grid=(t // tm,),
        in_specs=[row_spec(d), row_spec(d), row_spec(d), row_spec(PLE_DIM), vec, vec, vec, vec, vec,
                  _const_spec(wg.shape), _const_spec(wo.shape), _const_spec(wfi.shape), _const_spec(wfo.shape),
                  _const_spec(wpg.shape), _const_spec(wpl.shape)],
        out_specs=row_spec(d),
        out_shape=jax.ShapeDtypeStruct((t, d), F32),
        compiler_params=pltpu.CompilerParams(dimension_semantics=("parallel",), vmem_limit_bytes=VMEM_LIMIT),
        name="post",
    )(x2, ret2, nsa2, p2, gpre, gpost, gfpre, gfpost, gn, wg, wo, wfi, wfo, wpg, wpl)


def _pad_slabs(w, n, width):
    d = w.shape[0]
    return jnp.pad(w.reshape(d, n, width), ((0, 0), (0, 0), (0, LANES - width))).reshape(d, n * LANES)


def _layer(x, p, g_mix_pre, g_mix_post, g_ffn_pre, g_ffn_post, w_in, gn_g, pe_k, w1_k, w2_k, pe_v, w1_v, w2_v,
           w_out, w_ffn_in, w_ffn_out, w_ple, w_ple_gate):
    b, s, d = x.shape
    t = b * s
    g, hpg = NSA_GROUPS, NSA_HPG
    offs = [0]
    for n in IN_SPLITS:
        offs.append(offs[-1] + n)
    (w_rq, w_rk, w_rv, w_rg, w_nq, w_kc, w_vc, w_ks, w_vs, w_kw, w_vw, w_ng, w_mg) = [
        w_in[:, offs[i]:offs[i + 1]] for i in range(len(IN_SPLITS))]

    w_a = jnp.concatenate([
        w_rq, w_rk * (RET_DK ** -0.5), w_rv,
        _pad_slabs(w_nq * (LOG2E * NSA_DK ** -0.5), NSA_HEADS, NSA_DK),
        _pad_slabs(w_ks, g, NSA_DK), _pad_slabs(w_kw, g, NSA_DK),
        _pad_slabs(w_vs, g, NSA_DV), _pad_slabs(w_vw, g, NSA_DV)], axis=1).astype(BF16)
    w_ng_g = w_ng.reshape(d, 3, g, hpg).transpose(0, 2, 1, 3).reshape(d, g, 3 * hpg)
    w_f = jnp.concatenate([
        _pad_slabs(w_kc, g, NSA_DK), w_vc,
        jnp.pad(w_ng_g, ((0, 0), (0, 0), (0, LANES - 3 * hpg))).reshape(d, g * LANES)], axis=1).astype(BF16)

    x2 = x.reshape(t, d)
    gpre = g_mix_pre.reshape(1, d)
    pa, pf = _proj(x2, gpre, w_a, w_f, 512, 1280)
    pa3 = pa.reshape(b, s, PA_TOTAL)
    pf3 = pf.reshape(b, s, PF_TOTAL)

    ret = _retention(pa3)

    nh = s // CMP_STRIDE
    cend = jnp.arange(nh, dtype=jnp.int32) * CMP_STRIDE + (CMP_LEN - 1)
    ab = jnp.stack([cend // LANES] * 3 + [cend % LANES] * 3, axis=1).astype(F32)
    posc = jnp.zeros((nh, LANES), F32).at[:, ALIBI_COL:ALIBI_COL + 6].set(ab).at[:, BOUND_COL:BOUND_COL + 2].set(1.0)
    kcmp = _compress(pf3, PF_KC // LANES, 1, pe_k, w1_k, w2_k, posc)
    vcmp = _compress(pf3, PF_VC // LANES, 0, pe_v, w1_v, w2_v, jnp.zeros((nh, LANES), F32))

    nsa = _nsa(pa3, pf3, kcmp, vcmp)

    w_g = jnp.concatenate([w_rg, w_mg], axis=1).astype(BF16)
    out = _post(x2, ret.reshape(t, d), nsa.reshape(t, d), p.reshape(t, PLE_DIM),
                gpre, g_mix_post.reshape(1, d), g_ffn_pre.reshape(1, d), g_ffn_post.reshape(1, d),
                gn_g.reshape(1, d), w_g, w_out.astype(BF16), w_ffn_in.astype(BF16), w_ffn_out.astype(BF16),
                w_ple_gate.astype(BF16), w_ple.astype(BF16), 512, 256)
    return out.reshape(b, s, d)


def kernel(x, p, norm_mix_pre, norm_mix_post, norm_ffn_pre, norm_ffn_post, w_in, ret_gn_g, cmp_pe_k, cmp_w1_k,
           cmp_w2_k, cmp_pe_v, cmp_w1_v, cmp_w2_v, w_out, w_ffn_in, w_ffn_out, w_ple, w_ple_gate):
    for i in range(w_in.shape[0]):
        x = _layer(x, p[i], norm_mix_pre[i], norm_mix_post[i], norm_ffn_pre[i], norm_ffn_post[i], w_in[i],
                   ret_gn_g[i], cmp_pe_k[i], cmp_w1_k[i], cmp_w2_k[i], cmp_pe_v[i], cmp_w1_v[i], cmp_w2_v[i],
                   w_out[i], w_ffn_in[i], w_ffn_out[i], w_ple[i], w_ple_gate[i])
    return x
```

```python
import functools

import jax
import jax.numpy as jnp
from jax import lax
from jax.experimental import pallas as pl
from jax.experimental.pallas import tpu as pltpu

F32 = jnp.float32
BF16 = jnp.bfloat16

D_MODEL = 1024
PLE_DIM = 256
EPS = 1e-6
GN_EPS = 1e-5
NEG = -1e30
LOG2E = 1.4426950408889634

RET_HEADS = 8
RET_DV = D_MODEL // RET_HEADS
RET_DK = RET_DV // 2
RET_CHUNK = 128
RET_BATCH = 4

NSA_HEADS = 16
NSA_GROUPS = 2
NSA_HPG = NSA_HEADS // NSA_GROUPS
NSA_DV = D_MODEL // NSA_HEADS
NSA_DK = 3 * NSA_DV // 2
CMP_LEN = 32
CMP_STRIDE = 16
SEL_LEN = 64
SEL_TOPN = 16
WINDOW = 512
D_FF = -(-8 * D_MODEL // (3 * 256)) * 256

IN_SPLITS = (
    RET_HEADS * RET_DK, RET_HEADS * RET_DK, RET_HEADS * RET_DV, RET_HEADS * RET_DV,
    NSA_HEADS * NSA_DK, NSA_GROUPS * NSA_DK, NSA_GROUPS * NSA_DV, NSA_GROUPS * NSA_DK,
    NSA_GROUPS * NSA_DV, NSA_GROUPS * NSA_DK, NSA_GROUPS * NSA_DV, 3 * NSA_HEADS, 2 * D_MODEL,
)

LANES = 128
Q_TILE = 256
SEL_KEY_TILE = 256
WIN_KEYS = WINDOW + Q_TILE
ALIBI_COL = NSA_DK
BOUND_COL = ALIBI_COL + 6
DENOM_FLOOR = 2.0 ** -100
ONES_COL = NSA_DV
SUBLANES = 8
PAIR = LANES // NSA_DV
VMEM_LIMIT = 56 * 1024 * 1024

PA_RQ, PA_RK, PA_RV = 0, 512, 1024
PA_NQ = 2048
PA_KS = PA_NQ + NSA_HEADS * LANES
PA_KW = PA_KS + NSA_GROUPS * LANES
PA_VS = PA_KW + NSA_GROUPS * LANES
PA_VW = PA_VS + NSA_GROUPS * LANES
PA_TOTAL = PA_VW + NSA_GROUPS * LANES
PF_KC = 0
PF_VC = PF_KC + NSA_GROUPS * LANES
PF_NG = PF_VC + LANES
PF_TOTAL = PF_NG + NSA_GROUPS * LANES


def _dot(a, b):
    return jnp.dot(a, b, preferred_element_type=F32)


def _dot_nt(a, b):
    return lax.dot_general(a, b, (((1,), (1,)), ((), ())), preferred_element_type=F32)


def _dot_tn(a, b):
    return lax.dot_general(a, b, (((0,), (0,)), ((), ())), preferred_element_type=F32)


def _rms(x):
    return x * lax.rsqrt(jnp.mean(x * x, axis=-1, keepdims=True) + EPS)


def _const_spec(shape):
    nd = len(shape)
    return pl.BlockSpec(shape, lambda *_: (0,) * nd, pipeline_mode=pl.Buffered(1))


def _proj_kernel(x_ref, g_ref, wa_ref, wf_ref, oa_ref, of_ref, *, chunk):
    h = (_rms(x_ref[...]) * g_ref[...]).astype(BF16)
    for w_ref, o_ref in ((wa_ref, oa_ref), (wf_ref, of_ref)):
        n = o_ref.shape[-1]
        for c0 in range(0, n, chunk):
            c1 = min(c0 + chunk, n)
            o_ref[:, c0:c1] = _dot(h, w_ref[:, c0:c1]).astype(o_ref.dtype)


def _proj(x2, g, w_a, w_f, tm, chunk):
    t, d = x2.shape
    na, nf = w_a.shape[1], w_f.shape[1]
    return pl.pallas_call(
        functools.partial(_proj_kernel, chunk=chunk),
        grid=(t // tm,),
        in_specs=[pl.BlockSpec((tm, d), lambda i: (i, 0)), _const_spec((1, d)), _const_spec((d, na)),
                  _const_spec((d, nf))],
        out_specs=[pl.BlockSpec((tm, na), lambda i: (i, 0)), pl.BlockSpec((tm, nf), lambda i: (i, 0))],
        out_shape=[jax.ShapeDtypeStruct((t, na), BF16), jax.ShapeDtypeStruct((t, nf), F32)],
        compiler_params=pltpu.CompilerParams(dimension_semantics=("parallel",), vmem_limit_bytes=VMEM_LIMIT),
        name="proj",
    )(x2, g, w_a, w_f)


def _ret_kernel(q_ref, k_ref, v_ref, dmat_ref, xi_ref, zeta_ref, dec_ref, o_ref, state_ref):
    @pl.when(pl.program_id(1) == 0)
    def _():
        state_ref[...] = jnp.zeros_like(state_ref)

    nb = q_ref.shape[0]
    hs = [(i, h) for i in range(nb) for h in range(RET_HEADS)]
    q = [q_ref[i, :, h * RET_DK:(h + 1) * RET_DK] for i, h in hs]
    k = [k_ref[i, :, h * RET_DK:(h + 1) * RET_DK] for i, h in hs]
    v = [v_ref[i, :, h * RET_DV:(h + 1) * RET_DV] for i, h in hs]
    r = [state_ref[i, h] for i, h in hs]
    n = range(len(hs))
    inner = [_dot_nt(q[j], k[j]) for j in n]
    cross = [_dot(q[j], r[j].astype(BF16)) for j in n]
    kz = [(k[j].astype(F32) * zeta_ref[hs[j][1]]).astype(BF16) for j in n]
    upd = [_dot_tn(kz[j], v[j]) for j in n]
    inner = [(inner[j] * dmat_ref[hs[j][1]]).astype(BF16) for j in n]
    o = [_dot(inner[j], v[j]) + cross[j] * xi_ref[hs[j][1]] for j in n]
    for j, (i, h) in enumerate(hs):
        state_ref[i, h] = r[j] * dec_ref[h] + upd[j]
    mu = [jnp.mean(o[j], axis=-1, keepdims=True) for j in n]
    d = [o[j] - mu[j] for j in n]
    var = [jnp.mean(d[j] * d[j], axis=-1, keepdims=True) for j in n]
    for j, (i, h) in enumerate(hs):
        o_ref[i, :, h * RET_DV:(h + 1) * RET_DV] = (d[j] * lax.rsqrt(var[j] + GN_EPS)).astype(o_ref.dtype)


def _retention(pa3):
    b, s, _ = pa3.shape
    c = RET_CHUNK
    hh = RET_HEADS
    log_g = jnp.log1p(-(2.0 ** (-5.0 - jnp.arange(hh, dtype=F32))))
    idx = jnp.arange(c, dtype=F32)
    diff = idx[:, None] - idx[None, :]
    dmat = jnp.exp(log_g[:, None, None] * jnp.maximum(diff, 0.0)) * (diff >= 0)
    xi = jnp.broadcast_to(jnp.exp(log_g[:, None] * (idx + 1.0))[:, :, None], (hh, c, RET_DV))
    zeta = jnp.broadcast_to(jnp.exp(log_g[:, None] * (c - 1.0 - idx))[:, :, None], (hh, c, RET_DK))
    dec = jnp.broadcast_to(jnp.exp(log_g * c)[:, None, None], (hh, RET_DK, RET_DV))
    wq = hh * RET_DK
    wv = hh * RET_DV
    nb = RET_BATCH if b % RET_BATCH == 0 else 1
    return pl.pallas_call(
        _ret_kernel,
        grid=(b // nb, s // c),
        in_specs=[
            pl.BlockSpec((nb, c, wq), lambda i, j: (i, j, PA_RQ // wq)),
            pl.BlockSpec((nb, c, wq), lambda i, j: (i, j, PA_RK // wq)),
            pl.BlockSpec((nb, c, wv), lambda i, j: (i, j, PA_RV // wv)),
            _const_spec((hh, c, c)), _const_spec((hh, c, RET_DV)), _const_spec((hh, c, RET_DK)),
            _const_spec((hh, RET_DK, RET_DV)),
        ],
        out_specs=pl.BlockSpec((nb, c, wv), lambda i, j: (i, j, 0)),
        out_shape=jax.ShapeDtypeStruct((b, s, wv), BF16),
        scratch_shapes=[pltpu.VMEM((nb, hh, RET_DK, RET_DV), F32)],
        compiler_params=pltpu.CompilerParams(dimension_semantics=("parallel", "arbitrary")),
        name="retention",
    )(pa3, pa3, pa3, dmat, xi, zeta, dec)


def _cmp_kernel(src_ref, pe_ref, w1_ref, w2_ref, tab_ref, o_ref):
    n = o_ref.shape[2]
    top = jnp.zeros((n, LANES), F32)
    bot = jnp.zeros((n, LANES), F32)
    for l in range(CMP_STRIDE):
        x = src_ref[0, pl.ds(l, n, stride=CMP_STRIDE), :]
        top = top + _dot((x + pe_ref[0, l:l + 1, :]).astype(BF16), w1_ref[0, l])
        bot = bot + _dot((x + pe_ref[0, CMP_STRIDE + l:CMP_STRIDE + l + 1, :]).astype(BF16),
                         w1_ref[0, CMP_STRIDE + l])
    pre = top + pltpu.roll(bot, n - 1, 0)
    act = jax.nn.gelu(pre)
    o_ref[0, 0] = (_dot(act.astype(BF16), w2_ref[...]) + tab_ref[...]).astype(o_ref.dtype)


def _compress(pf3, slab0, slab_step, pe, w1, w2, tab):
    b, s, _ = pf3.shape
    g = NSA_GROUPS
    n = s // CMP_STRIDE
    d = w2.shape[0]
    lane_step = 0 if slab_step else d
    pe_g = jnp.stack([jnp.pad(pe, ((0, 0), (j * lane_step, LANES - d - j * lane_step))) for j in range(g)])
    w1_g = jnp.stack([jnp.pad(w1.reshape(CMP_LEN, d, d), ((0, 0), (j * lane_step, LANES - d - j * lane_step),
                                                            (0, LANES - d))) for j in range(g)]).astype(BF16)
    w2p = jnp.pad(w2, ((0, LANES - d), (0, LANES - d))).astype(BF16)
    return pl.pallas_call(
        _cmp_kernel,
        grid=(b, g),
        in_specs=[pl.BlockSpec((1, s, LANES), lambda i, j: (i, 0, slab0 + j * slab_step)),
                  pl.BlockSpec((1, CMP_LEN, LANES), lambda i, j: (j, 0, 0)),
                  pl.BlockSpec((1, CMP_LEN, LANES, LANES), lambda i, j: (j, 0, 0, 0)),
                  _const_spec((LANES, LANES)), _const_spec((n, LANES))],
        out_specs=pl.BlockSpec((1, 1, n, LANES), lambda i, j: (i, j, 0, 0)),
        out_shape=jax.ShapeDtypeStruct((b, g, n, LANES), BF16),
        compiler_params=pltpu.CompilerParams(dimension_semantics=("parallel", "parallel")),
        name="compress_%d" % d,
    )(pf3, pe_g, w1_g, w2p, tab)


def _split3(x):
    p1 = x.astype(BF16)
    r1 = x - p1.astype(F32)
    p2 = r1.astype(BF16)
    p3 = (r1 - p2.astype(F32)).astype(BF16)
    return p1, p2, p3


def _nsa_body(q_ref, ng_ref, kc_ref, vc_ref, ks_ref, kw_ref, vs_ref, vw_ref, posk_ref, oh_ref, qal_ref,
              slope_ref, ovt_ref, ones_ref, o_ref, m_ref, acc_ref, s_ref, mx_ref, part_ref, cmax_ref, *, topn, fast):
    hpg = NSA_HPG
    tq = Q_TILE
    t0 = pl.program_id(2) * tq
    n_sel = ovt_ref.shape[0]
    n_cmp = kc_ref.shape[2]
    pair_rows = PAIR * tq
    n_pairs = hpg // PAIR
    bounds = [(c * pair_rows, (c + 1) * pair_rows) for c in range(n_pairs)]
    lane_q = lax.broadcasted_iota(jnp.int32, (tq, LANES), 1)

    q_all = q_ref[0]
    q_heads = [q_all[:, h * LANES:(h + 1) * LANES] for h in range(hpg)]
    if fast:
        t_abs = (t0 + lax.broadcasted_iota(jnp.int32, (tq, LANES), 0)).astype(F32)
        cmax = cmax_ref[...]
        spread = [_dot(jnp.abs(q_heads[h]), cmax) for h in range(hpg)]
        m_up = [spread[h] + slope_ref[0, h * tq:(h + 1) * tq, :] * t_abs + 1.0 for h in range(hpg)]
        m1 = [m.astype(BF16) for m in m_up]
        m2 = [(m_up[h] - m1[h].astype(F32)).astype(BF16) for h in range(hpg)]
        q_heads = [jnp.where(lane_q == BOUND_COL, -m1[h], jnp.where(lane_q == BOUND_COL + 1, -m2[h], q_heads[h]))
                   for h in range(hpg)]
    q128 = jnp.concatenate(q_heads, axis=0) + qal_ref[0]

    def softmax_num(s):
        return jnp.exp2(s) if fast else jnp.exp2(s - jnp.max(s, axis=-1, keepdims=True))

    t_c = t0 + lax.broadcasted_iota(jnp.int32, (tq, n_cmp), 0)
    end_c = lax.broadcasted_iota(jnp.int32, (tq, n_cmp), 1) * CMP_STRIDE + (CMP_LEN - 1)
    bias_c = jnp.where(end_c <= t_c, 0.0, NEG)[None]
    seen = (t0 + lax.broadcasted_iota(jnp.int32, (tq, 1), 0)) >= CMP_LEN - 1
    kc = kc_ref[0, 0]
    vc = vc_ref[0, 0] + ones_ref[0:n_cmp, :]
    psum = jnp.zeros((tq, n_cmp), F32)
    l_min = jnp.full((PAIR, tq, 1), 1.0, F32)
    acc_c = []
    for r0, r1 in bounds:
        e_c = softmax_num(_dot_nt(q128[r0:r1], kc).reshape(PAIR, tq, n_cmp) + bias_c)
        acc_c.append(_dot(e_c.reshape(pair_rows, n_cmp).astype(BF16), vc))
        l_c = jnp.sum(e_c, axis=-1, keepdims=True)
        l_min = jnp.minimum(l_min, jnp.where(seen[None], l_c, 1.0))
        psum = psum + jnp.sum(e_c * jnp.where(seen[None], 1.0 / l_c, 0.0), axis=0)

    ovt = ovt_ref[...]
    imp_t = sum(_dot_nt(ovt, piece) for piece in _split3(psum))
    jj = lax.broadcasted_iota(jnp.int32, (n_sel, tq), 0)
    tt = t0 + lax.broadcasted_iota(jnp.int32, (n_sel, tq), 1)
    cur = tt // SEL_LEN
    forced = (jj == 0) | (jj == cur) | (jj == cur - 1)
    score = jnp.where(forced, 1e6, jnp.where(jj * SEL_LEN <= tt, imp_t, -1e6))
    sub = lax.broadcasted_iota(jnp.int32, (SUBLANES, tq), 0)
    groups = [score[SUBLANES * v:SUBLANES * (v + 1), :] for v in range(n_sel // SUBLANES)]
    ranks = [jnp.zeros((SUBLANES, tq), F32) for _ in groups]
    for i in range(n_sel):
        vi, ri = divmod(i, SUBLANES)
        si = jnp.broadcast_to(groups[vi][ri:ri + 1, :], (SUBLANES, tq))
        for v, sv in enumerate(groups):
            if v > vi:
                before = jnp.where(si >= sv, 1.0, 0.0)
            elif v < vi:
                before = jnp.where(si > sv, 1.0, 0.0)
            else:
                before = jnp.where(sub > ri, jnp.where(si >= sv, 1.0, 0.0), jnp.where(si > sv, 1.0, 0.0))
            ranks[v] = ranks[v] + before
    pen_t = jnp.where(jnp.concatenate(ranks, axis=0) < float(topn), 0.0, NEG)
    pen_t = jnp.concatenate([pen_t, jnp.zeros((LANES - n_sel, tq), F32)], axis=0)
    pen = pen_t.T.astype(BF16)
    q256 = jnp.concatenate([q128, jnp.concatenate([pen] * hpg, axis=0)], axis=1)

    gate = jax.nn.sigmoid(ng_ref[0])
    low = lane_q < NSA_DV
    l_acc = [jnp.full((pair_rows, LANES), 1.0, F32)]

    def gated(acc_pair, branch, c):
        if branch:
            l_acc[0] = jnp.minimum(l_acc[0], jnp.where(jnp.concatenate([low] * PAIR, axis=0), 1.0, acc_pair))
        a_e, a_o = acc_pair[:tq], acc_pair[tq:]
        keep = jnp.where(low, a_e, a_o)
        swap = pltpu.roll(jnp.where(low, a_o, a_e), NSA_DV, 1)
        col = branch * hpg + PAIR * c
        g2 = jnp.where(low, gate[:, col:col + 1], gate[:, col + 1:col + 2])
        return g2 * jnp.where(low, keep, swap) / jnp.where(low, swap, keep)

    w0 = pl.multiple_of(jnp.maximum(t0 - WINDOW, 0), LANES)
    kw = kw_ref[0, pl.ds(w0, WIN_KEYS), :] + posk_ref[pl.ds(w0, WIN_KEYS), :]
    vw = vw_ref[0, pl.ds(w0, WIN_KEYS), :] + ones_ref[...]
    pos_w = w0 + lax.broadcasted_iota(jnp.int32, (tq, WIN_KEYS), 1)
    t_w = t0 + lax.broadcasted_iota(jnp.int32, (tq, WIN_KEYS), 0)
    bias_w = jnp.where((pos_w <= t_w) & (pos_w > t_w - WINDOW), 0.0, NEG)[None]
    for c, (r0, r1) in enumerate(bounds):
        e_w = softmax_num(_dot_nt(q128[r0:r1], kw).reshape(PAIR, tq, WIN_KEYS) + bias_w)
        acc_w = _dot(e_w.reshape(pair_rows, WIN_KEYS).astype(BF16), vw)
        part_ref[:, c * LANES:(c + 1) * LANES] = jnp.where(seen, gated(acc_c[c], 0, c), 0.0) + gated(acc_w, 2, c)

    tk = SEL_KEY_TILE
    ones_v = ones_ref[0:tk, :]
    n_full = t0 // tk

    def keys(kt):
        k0 = pl.multiple_of(kt * tk, tk)
        return jnp.concatenate([ks_ref[0, pl.ds(k0, tk), :] + posk_ref[pl.ds(k0, tk), :],
                                oh_ref[pl.ds(k0, tk), :]], axis=1)

    def logits(kt):
        return _dot_nt(q256, keys(kt))

    def values(kt):
        return vs_ref[0, pl.ds(pl.multiple_of(kt * tk, tk), tk), :] + ones_v

    def row_max(s):
        return jnp.broadcast_to(jnp.max(s, axis=-1, keepdims=True), (s.shape[0], LANES))

    pos_d = n_full * tk + lax.broadcasted_iota(jnp.int32, (tq, tk), 1)
    t_d = t0 + lax.broadcasted_iota(jnp.int32, (tq, tk), 0)
    bias_d = jnp.where(pos_d <= t_d, 0.0, NEG)[None]
    k_d, v_d = keys(n_full), values(n_full)
    for r0, r1 in bounds:
        s_d = (_dot_nt(q256[r0:r1], k_d).reshape(PAIR, tq, tk) + bias_d).reshape(pair_rows, tk)
        if fast:
            p_d = jnp.exp2(s_d)
        else:
            m_d = row_max(s_d)
            m_ref[r0:r1] = m_d
            p_d = jnp.exp2(s_d - jnp.concatenate([m_d] * (tk // LANES), axis=1))
        acc_ref[r0:r1] = _dot(p_d.astype(BF16), v_d)

    if fast:
        s_ref[0] = logits(0)
        last = jnp.maximum(n_full - 1, 0)

        def body(j, carry):
            ka, kb, kc2 = 2 * j, jnp.minimum(2 * j + 1, last), jnp.minimum(2 * j + 2, last)
            s_a = s_ref[0]
            s_ref[1] = logits(kb)
            acc_ref[...] = acc_ref[...] + _dot(jnp.exp2(s_a).astype(BF16), values(ka))
            s_b = s_ref[1]
            s_ref[0] = logits(kc2)
            v_b = values(kb)
            v_b = jnp.where(2 * j + 1 < n_full, v_b, jnp.zeros_like(v_b))
            acc_ref[...] = acc_ref[...] + _dot(jnp.exp2(s_b).astype(BF16), v_b)
            return carry

        n_steps = (n_full + 1) // 2
    else:
        k_0, k_1 = keys(0), keys(jnp.minimum(1, n_full))
        for r0, r1 in bounds:
            s_first = _dot_nt(q256[r0:r1], k_0)
            s_ref[0, r0:r1] = s_first
            mx_ref[0, r0:r1] = row_max(s_first)
            s_ref[1, r0:r1] = _dot_nt(q256[r0:r1], k_1)

        def body(kt, carry):
            s = s_ref[kt % 3]
            m_cur = mx_ref[kt % 2]
            mx_next = row_max(s_ref[(kt + 1) % 3])
            s_next = logits(jnp.minimum(kt + 2, n_full))
            m_prev = m_ref[...]
            m_new = jnp.maximum(m_prev, m_cur)
            p = jnp.exp2(s - jnp.concatenate([m_new] * (tk // LANES), axis=1))
            acc_ref[...] = acc_ref[...] * jnp.exp2(m_prev - m_new) + _dot(p.astype(BF16), values(kt))
            m_ref[...] = m_new
            mx_ref[(kt + 1) % 2] = mx_next
            s_ref[(kt + 2) % 3] = s_next
            return carry

        n_steps = n_full

    lax.fori_loop(0, n_steps, body, 0)
    for c, (r0, r1) in enumerate(bounds):
        sel = gated(acc_ref[r0:r1], 1, c)
        o_ref[0, :, c * LANES:(c + 1) * LANES] = (part_ref[:, c * LANES:(c + 1) * LANES] + sel).astype(o_ref.dtype)
    return jnp.minimum(jnp.min(l_min), jnp.min(l_acc[0]))


def _nsa_kernel(q_ref, ng_ref, kc_ref, vc_ref, ks_ref, kw_ref, vs_ref, vw_ref, posk_ref, oh_ref, qal_ref, slope_ref,
                ovt_ref, ones_ref, o_ref, m_ref, acc_ref, s_ref, mx_ref, part_ref, cmax_ref, *, topn):
    refs = (q_ref, ng_ref, kc_ref, vc_ref, ks_ref, kw_ref, vs_ref, vw_ref, posk_ref, oh_ref, qal_ref, slope_ref,
            ovt_ref, ones_ref, o_ref, m_ref, acc_ref, s_ref, mx_ref, part_ref, cmax_ref)

    @pl.when(pl.program_id(2) == 0)
    def _():
        def col_max(k):
            return jnp.max(jnp.abs(k.astype(F32)), axis=0, keepdims=True)

        cm = jnp.maximum(jnp.maximum(col_max(ks_ref[0]), col_max(kw_ref[0])), col_max(kc_ref[0, 0]))
        feat = lax.broadcasted_iota(jnp.int32, (1, LANES), 1) < NSA_DK
        cmax_ref[...] = jnp.broadcast_to(jnp.where(feat, cm, 0.0), (LANES, LANES)).T.astype(BF16)

    l_min = _nsa_body(*refs, topn=topn, fast=True)

    @pl.when(jnp.logical_not(l_min > DENOM_FLOOR))
    def _():
        _nsa_body(*refs, topn=topn, fast=False)


def _nsa(pa3, pf3, kcmp, vcmp):
    b, s, _ = pa3.shape
    g, hpg = NSA_GROUPS, NSA_HPG
    n_sel = s // SEL_LEN
    n_cmp = kcmp.shape[2]
    topn = min(SEL_TOPN, n_sel)
    rows = hpg * Q_TILE

    pos = jnp.arange(s, dtype=jnp.int32)
    ab = jnp.stack([pos // LANES] * 3 + [pos % LANES] * 3, axis=1).astype(F32)
    posk = jnp.zeros((s, LANES), F32).at[:, ALIBI_COL:ALIBI_COL + 6].set(ab)
    posk = posk.at[:, BOUND_COL:BOUND_COL + 2].set(1.0).astype(BF16)
    onehot = (pos[:, None] // SEL_LEN == jnp.arange(LANES)[None, :]).astype(BF16)
    slopes = LOG2E * 2.0 ** (-8.0 * (jnp.arange(NSA_HEADS, dtype=F32) + 1.0) / NSA_HEADS)
    s1, s2, s3 = _split3(slopes)
    qcols = jnp.stack([s1.astype(F32) * LANES, s2.astype(F32) * LANES, s3.astype(F32) * LANES,
                       s1.astype(F32), s2.astype(F32), s3.astype(F32)], axis=1)
    qal = jnp.zeros((NSA_HEADS, LANES), F32).at[:, ALIBI_COL:ALIBI_COL + 6].set(qcols)
    qal = jnp.broadcast_to(qal.reshape(g, hpg, 1, LANES), (g, hpg, Q_TILE, LANES)).reshape(g, rows, LANES)
    qal = qal.astype(BF16)
    slope_rows = jnp.broadcast_to(slopes.reshape(g, hpg, 1, 1), (g, hpg, Q_TILE, LANES)).reshape(g, rows, LANES)
    cstart = jnp.arange(n_cmp) * CMP_STRIDE
    sstart = jnp.arange(n_sel) * SEL_LEN
    ovt = ((cstart[None, :] < sstart[:, None] + SEL_LEN) & (cstart[None, :] + CMP_LEN > sstart[:, None])
           & (jnp.arange(n_cmp)[None, :] < n_cmp - 1)).astype(BF16)
    ones = jnp.zeros((WIN_KEYS, LANES), BF16).at[:, ONES_COL:].set(1.0)

    def seq_spec(col0):
        return pl.BlockSpec((1, s, LANES), lambda i, j, t: (i, 0, col0 // LANES + j))

    return pl.pallas_call(
        functools.partial(_nsa_kernel, topn=topn),
        grid=(b, g, s // Q_TILE),
        in_specs=[
            pl.BlockSpec((1, Q_TILE, hpg * LANES), lambda i, j, t: (i, t, PA_NQ // (hpg * LANES) + j)),
            pl.BlockSpec((1, Q_TILE, LANES), lambda i, j, t: (i, t, PF_NG // LANES + j)),
            pl.BlockSpec((1, 1, n_cmp, LANES), lambda i, j, t: (i, j, 0, 0)),
            pl.BlockSpec((1, 1, n_cmp, LANES), lambda i, j, t: (i, j, 0, 0)),
            seq_spec(PA_KS), seq_spec(PA_KW), seq_spec(PA_VS), seq_spec(PA_VW),
            _const_spec((s, LANES)), _const_spec((s, LANES)),
            pl.BlockSpec((1, rows, LANES), lambda i, j, t: (j, 0, 0)),
            pl.BlockSpec((1, rows, LANES), lambda i, j, t: (j, 0, 0)),
            _const_spec((n_sel, n_cmp)), _const_spec((WIN_KEYS, LANES)),
        ],
        out_specs=pl.BlockSpec((1, Q_TILE, hpg * NSA_DV), lambda i, j, t: (i, t, j)),
        out_shape=jax.ShapeDtypeStruct((b, s, NSA_HEADS * NSA_DV), BF16),
        scratch_shapes=[pltpu.VMEM((rows, LANES), F32), pltpu.VMEM((rows, LANES), F32),
                        pltpu.VMEM((3, rows, SEL_KEY_TILE), F32), pltpu.VMEM((2, rows, LANES), F32),
                        pltpu.VMEM((Q_TILE, hpg * NSA_DV), F32), pltpu.VMEM((LANES, LANES), BF16)],
        compiler_params=pltpu.CompilerParams(dimension_semantics=("parallel", "parallel", "arbitrary"),
                                             vmem_limit_bytes=VMEM_LIMIT),
        name="nsa_attention",
    )(pa3, pf3, kcmp, vcmp, pa3, pa3, pa3, pa3, posk, onehot, qal, slope_rows, ovt, ones)


def _post_kernel(x_ref, ret_ref, nsa_ref, p_ref, gpre_ref, gpost_ref, gfpre_ref, gfpost_ref, gn_ref,
                 wg_ref, wo_ref, wfi_ref, wfo_ref, wpg_ref, wpl_ref, o_ref, *, ff_chunk):
    d = D_MODEL
    x = x_ref[...]
    h = (_rms(x) * gpre_ref[...]).astype(BF16)
    ret = jax.nn.silu(_dot(h, wg_ref[:, 0:d])) * (ret_ref[...].astype(F32) * gn_ref[...])
    mixin = jax.nn.sigmoid(_dot(h, wg_ref[:, d:2 * d])) * ret
    mixin = mixin + jax.nn.sigmoid(_dot(h, wg_ref[:, 2 * d:3 * d])) * nsa_ref[...].astype(F32)
    x1 = x + _rms(_dot(mixin.astype(BF16), wo_ref[...])) * gpost_ref[...]

    h2 = (_rms(x1) * gfpre_ref[...]).astype(BF16)
    f = jnp.zeros(x.shape, F32)
    for c0 in range(0, D_FF, ff_chunk):
        u = _dot(h2, wfi_ref[:, c0:c0 + ff_chunk])
        v = _dot(h2, wfi_ref[:, D_FF + c0:D_FF + c0 + ff_chunk])
        f = f + _dot((jax.nn.silu(u) * v).astype(BF16), wfo_ref[c0:c0 + ff_chunk, :])
    x2 = x1 + _rms(f) * gfpost_ref[...]

    gate = jax.nn.sigmoid(_dot(_rms(x2).astype(BF16), wpg_ref[...]))
    o_ref[...] = x2 + gate * _dot(p_ref[...].astype(BF16), wpl_ref[...])


def _post(x2, ret2, nsa2, p2, gpre, gpost, gfpre, gfpost, gn, wg, wo, wfi, wfo, wpg, wpl, tm, ff_chunk):
    t, d = x2.shape

    def row_spec(w):
        return pl.BlockSpec((tm, w), lambda i: (i, 0))

    vec = _const_spec((1, d))
    return pl.pallas_call(
        functools.partial(_post_kernel, ff_chunk=ff_chunk),
        grid=(t // tm,),
        in_specs=[row_spec(d), row_spec(d), row_spec(d), row_spec(PLE_DIM), vec, vec, vec, vec, vec,
                  _const_spec(wg.shape), _const_spec(wo.shape), _const_spec(wfi.shape), _const_spec(wfo.shape),
                  _const_spec(wpg.shape), _const_spec(wpl.shape)],
        out_specs=row_spec(d),
        out_shape=jax.ShapeDtypeStruct((t, d), F32),
        compiler_params=pltpu.CompilerParams(dimension_semantics=("parallel",), vmem_limit_bytes=VMEM_LIMIT),
        name="post",
    )(x2, ret2, nsa2, p2, gpre, gpost, gfpre, gfpost, gn, wg, wo, wfi, wfo, wpg, wpl)


def _pad_slabs(w, n, width):
    d = w.shape[0]
    return jnp.pad(w.reshape(d, n, width), ((0, 0), (0, 0), (0, LANES - width))).reshape(d, n * LANES)


def _layer(x, p, g_mix_pre, g_mix_post, g_ffn_pre, g_ffn_post, w_in, gn_g, pe_k, w1_k, w2_k, pe_v, w1_v, w2_v,
           w_out, w_ffn_in, w_ffn_out, w_ple, w_ple_gate):
    b, s, d = x.shape
    t = b * s
    g, hpg = NSA_GROUPS, NSA_HPG
    offs = [0]
    for n in IN_SPLITS:
        offs.append(offs[-1] + n)
    (w_rq, w_rk, w_rv, w_rg, w_nq, w_kc, w_vc, w_ks, w_vs, w_kw, w_vw, w_ng, w_mg) = [
        w_in[:, offs[i]:offs[i + 1]] for i in range(len(IN_SPLITS))]

    w_a = jnp.concatenate([
        w_rq, w_rk * (RET_DK ** -0.5), w_rv,
        _pad_slabs(w_nq * (LOG2E * NSA_DK ** -0.5), NSA_HEADS, NSA_DK),
        _pad_slabs(w_ks, g, NSA_DK), _pad_slabs(w_kw, g, NSA_DK),
        _pad_slabs(w_vs, g, NSA_DV), _pad_slabs(w_vw, g, NSA_DV)], axis=1).astype(BF16)
    w_ng_g = w_ng.reshape(d, 3, g, hpg).transpose(0, 2, 1, 3).reshape(d, g, 3 * hpg)
    w_f = jnp.concatenate([
        _pad_slabs(w_kc, g, NSA_DK), w_vc,
        jnp.pad(w_ng_g, ((0, 0), (0, 0), (0, LANES - 3 * hpg))).reshape(d, g * LANES)], axis=1).astype(BF16)

    x2 = x.reshape(t, d)
    gpre = g_mix_pre.reshape(1, d)
    pa, pf = _proj(x2, gpre, w_a, w_f, 512, 1280)
    pa3 = pa.reshape(b, s, PA_TOTAL)
    pf3 = pf.reshape(b, s, PF_TOTAL)

    ret = _retention(pa3)

    nh = s // CMP_STRIDE
    cend = jnp.arange(nh, dtype=jnp.int32) * CMP_STRIDE + (CMP_LEN - 1)
    ab = jnp.stack([cend // LANES] * 3 + [cend % LANES] * 3, axis=1).astype(F32)
    posc = jnp.zeros((nh, LANES), F32).at[:, ALIBI_COL:ALIBI_COL + 6].set(ab).at[:, BOUND_COL:BOUND_COL + 2].set(1.0)
    kcmp = _compress(pf3, PF_KC // LANES, 1, pe_k, w1_k, w2_k, posc)
    vcmp = _compress(pf3, PF_VC // LANES, 0, pe_v, w1_v, w2_v, jnp.zeros((nh, LANES), F32))

    nsa = _nsa(pa3, pf3, kcmp, vcmp)

    w_g = jnp.concatenate([w_rg, w_mg], axis=1).astype(BF16)
    out = _post(x2, ret.reshape(t, d), nsa.reshape(t, d), p.reshape(t, PLE_DIM),
                gpre, g_mix_post.reshape(1, d), g_ffn_pre.reshape(1, d), g_ffn_post.reshape(1, d),
                gn_g.reshape(1, d), w_g, w_out.astype(BF16), w_ffn_in.astype(BF16), w_ffn_out.astype(BF16),
                w_ple_gate.astype(BF16), w_ple.astype(BF16), 512, 256)
    return out.reshape(b, s, d)


def kernel(x, p, norm_mix_pre, norm_mix_post, norm_ffn_pre, norm_ffn_post, w_in, ret_gn_g, cmp_pe_k, cmp_w1_k,
           cmp_w2_k, cmp_pe_v, cmp_w1_v, cmp_w2_v, w_out, w_ffn_in, w_ffn_out, w_ple, w_ple_gate):
    for i in range(w_in.shape[0]):
        x = _layer(x, p[i], norm_mix_pre[i], norm_mix_post[i], norm_ffn_pre[i], norm_ffn_post[i], w_in[i],
                   ret_gn_g[i], cmp_pe_k[i], cmp_w1_k[i], cmp_w2_k[i], cmp_pe_v[i], cmp_w1_v[i], cmp_w2_v[i],
                   w_out[i], w_ffn_in[i], w_ffn_out[i], w_ple[i], w_ple_gate[i])
    return x
```

```python
import functools

import jax
import jax.numpy as jnp
from jax import lax
from jax.experimental import pallas as pl
from jax.experimental.pallas import tpu as pltpu

F32 = jnp.float32
BF16 = jnp.bfloat16

D_MODEL = 1024
PLE_DIM = 256
EPS = 1e-6
GN_EPS = 1e-5
NEG = -1e30
LOG2E = 1.4426950408889634

RET_HEADS = 8
RET_DV = D_MODEL // RET_HEADS
RET_DK = RET_DV // 2
RET_CHUNK = 128
RET_BATCH = 4

NSA_HEADS = 16
NSA_GROUPS = 2
NSA_HPG = NSA_HEADS // NSA_GROUPS
NSA_DV = D_MODEL // NSA_HEADS
NSA_DK = 3 * NSA_DV // 2
CMP_LEN = 32
CMP_STRIDE = 16
SEL_LEN = 64
SEL_TOPN = 16
WINDOW = 512
D_FF = -(-8 * D_MODEL // (3 * 256)) * 256

IN_SPLITS = (
    RET_HEADS * RET_DK, RET_HEADS * RET_DK, RET_HEADS * RET_DV, RET_HEADS * RET_DV,
    NSA_HEADS * NSA_DK, NSA_GROUPS * NSA_DK, NSA_GROUPS * NSA_DV, NSA_GROUPS * NSA_DK,
    NSA_GROUPS * NSA_DV, NSA_GROUPS * NSA_DK, NSA_GROUPS * NSA_DV, 3 * NSA_HEADS, 2 * D_MODEL,
)

LANES = 128
Q_TILE = 256
SEL_KEY_TILE = 256
WIN_KEYS = WINDOW + Q_TILE
ALIBI_COL = NSA_DK
BOUND_COL = ALIBI_COL + 6
DENOM_FLOOR = 2.0 ** -100
ONES_COL = NSA_DV
SUBLANES = 8
PAIR = LANES // NSA_DV
VMEM_LIMIT = 56 * 1024 * 1024

PA_RQ, PA_RK, PA_RV = 0, 512, 1024
PA_NQ = 2048
PA_KS = PA_NQ + NSA_HEADS * LANES
PA_KW = PA_KS + NSA_GROUPS * LANES
PA_VS = PA_KW + NSA_GROUPS * LANES
PA_VW = PA_VS + NSA_GROUPS * LANES
PA_TOTAL = PA_VW + NSA_GROUPS * LANES
PF_KC = 0
PF_VC = PF_KC + NSA_GROUPS * LANES
PF_NG = PF_VC + LANES
PF_TOTAL = PF_NG + NSA_GROUPS * LANES


def _dot(a, b):
    return jnp.dot(a, b, preferred_element_type=F32)


def _dot_nt(a, b):
    return lax.dot_general(a, b, (((1,), (1,)), ((), ())), preferred_element_type=F32)


def _dot_tn(a, b):
    return lax.dot_general(a, b, (((0,), (0,)), ((), ())), preferred_element_type=F32)


def _rms(x):
    return x * lax.rsqrt(jnp.mean(x * x, axis=-1, keepdims=True) + EPS)


def _const_spec(shape):
    nd = len(shape)
    return pl.BlockSpec(shape, lambda *_: (0,) * nd, pipeline_mode=pl.Buffered(1))


def _proj_kernel(x_ref, g_ref, wa_ref, wf_ref, oa_ref, of_ref, *, chunk):
    h = (_rms(x_ref[...]) * g_ref[...]).astype(BF16)
    for w_ref, o_ref in ((wa_ref, oa_ref), (wf_ref, of_ref)):
        n = o_ref.shape[-1]
        for c0 in range(0, n, chunk):
            c1 = min(c0 + chunk, n)
            o_ref[:, c0:c1] = _dot(h, w_ref[:, c0:c1]).astype(o_ref.dtype)


def _proj(x2, g, w_a, w_f, tm, chunk):
    t, d = x2.shape
    na, nf = w_a.shape[1], w_f.shape[1]
    return pl.pallas_call(
        functools.partial(_proj_kernel, chunk=chunk),
        grid=(t // tm,),
        in_specs=[pl.BlockSpec((tm, d), lambda i: (i, 0)), _const_spec((1, d)), _const_spec((d, na)),
                  _const_spec((d, nf))],
        out_specs=[pl.BlockSpec((tm, na), lambda i: (i, 0)), pl.BlockSpec((tm, nf), lambda i: (i, 0))],
        out_shape=[jax.ShapeDtypeStruct((t, na), BF16), jax.ShapeDtypeStruct((t, nf), F32)],
        compiler_params=pltpu.CompilerParams(dimension_semantics=("parallel",), vmem_limit_bytes=VMEM_LIMIT),
        name="proj",
    )(x2, g, w_a, w_f)


def _ret_kernel(q_ref, k_ref, v_ref, dmat_ref, xi_ref, zeta_ref, dec_ref, o_ref, state_ref):
    @pl.when(pl.program_id(1) == 0)
    def _():
        state_ref[...] = jnp.zeros_like(state_ref)

    nb = q_ref.shape[0]
    hs = [(i, h) for i in range(nb) for h in range(RET_HEADS)]
    q = [q_ref[i, :, h * RET_DK:(h + 1) * RET_DK] for i, h in hs]
    k = [k_ref[i, :, h * RET_DK:(h + 1) * RET_DK] for i, h in hs]
    v = [v_ref[i, :, h * RET_DV:(h + 1) * RET_DV] for i, h in hs]
    r = [state_ref[i, h] for i, h in hs]
    n = range(len(hs))
    inner = [_dot_nt(q[j], k[j]) for j in n]
    cross = [_dot(q[j], r[j].astype(BF16)) for j in n]
    kz = [(k[j].astype(F32) * zeta_ref[hs[j][1]]).astype(BF16) for j in n]
    upd = [_dot_tn(kz[j], v[j]) for j in n]
    inner = [(inner[j] * dmat_ref[hs[j][1]]).astype(BF16) for j in n]
    o = [_dot(inner[j], v[j]) + cross[j] * xi_ref[hs[j][1]] for j in n]
    for j, (i, h) in enumerate(hs):
        state_ref[i, h] = r[j] * dec_ref[h] + upd[j]
    mu = [jnp.mean(o[j], axis=-1, keepdims=True) for j in n]
    d = [o[j] - mu[j] for j in n]
    var = [jnp.mean(d[j] * d[j], axis=-1, keepdims=True) for j in n]
    for j, (i, h) in enumerate(hs):
        o_ref[i, :, h * RET_DV:(h + 1) * RET_DV] = (d[j] * lax.rsqrt(var[j] + GN_EPS)).astype(o_ref.dtype)


def _retention(pa3):
    b, s, _ = pa3.shape
    c = RET_CHUNK
    hh = RET_HEADS
    log_g = jnp.log1p(-(2.0 ** (-5.0 - jnp.arange(hh, dtype=F32))))
    idx = jnp.arange(c, dtype=F32)
    diff = idx[:, None] - idx[None, :]
    dmat = jnp.exp(log_g[:, None, None] * jnp.maximum(diff, 0.0)) * (diff >= 0)
    xi = jnp.broadcast_to(jnp.exp(log_g[:, None] * (idx + 1.0))[:, :, None], (hh, c, RET_DV))
    zeta = jnp.broadcast_to(jnp.exp(log_g[:, None] * (c - 1.0 - idx))[:, :, None], (hh, c, RET_DK))
    dec = jnp.broadcast_to(jnp.exp(log_g * c)[:, None, None], (hh, RET_DK, RET_DV))
    wq = hh * RET_DK
    wv = hh * RET_DV
    nb = RET_BATCH if b % RET_BATCH == 0 else 1
    return pl.pallas_call(
        _ret_kernel,
        grid=(b // nb, s // c),
        in_specs=[
            pl.BlockSpec((nb, c, wq), lambda i, j: (i, j, PA_RQ // wq)),
            pl.BlockSpec((nb, c, wq), lambda i, j: (i, j, PA_RK // wq)),
            pl.BlockSpec((nb, c, wv), lambda i, j: (i, j, PA_RV // wv)),
            _const_spec((hh, c, c)), _const_spec((hh, c, RET_DV)), _const_spec((hh, c, RET_DK)),
            _const_spec((hh, RET_DK, RET_DV)),
        ],
        out_specs=pl.BlockSpec((nb, c, wv), lambda i, j: (i, j, 0)),
        out_shape=jax.ShapeDtypeStruct((b, s, wv), BF16),
        scratch_shapes=[pltpu.VMEM((nb, hh, RET_DK, RET_DV), F32)],
        compiler_params=pltpu.CompilerParams(dimension_semantics=("parallel", "arbitrary")),
        name="retention",
    )(pa3, pa3, pa3, dmat, xi, zeta, dec)


def _cmp_kernel(src_ref, pe_ref, w1_ref, w2_ref, tab_ref, o_ref):
    n = o_ref.shape[2]
    top = jnp.zeros((n, LANES), F32)
    bot = jnp.zeros((n, LANES), F32)
    for l in range(CMP_STRIDE):
        x = src_ref[0, pl.ds(l, n, stride=CMP_STRIDE), :]
        top = top + _dot((x + pe_ref[0, l:l + 1, :]).astype(BF16), w1_ref[0, l])
        bot = bot + _dot((x + pe_ref[0, CMP_STRIDE + l:CMP_STRIDE + l + 1, :]).astype(BF16),
                         w1_ref[0, CMP_STRIDE + l])
    pre = top + pltpu.roll(bot, n - 1, 0)
    act = jax.nn.gelu(pre)
    o_ref[0, 0] = (_dot(act.astype(BF16), w2_ref[...]) + tab_ref[...]).astype(o_ref.dtype)


def _compress(pf3, slab0, slab_step, pe, w1, w2, tab):
    b, s, _ = pf3.shape
    g = NSA_GROUPS
    n = s // CMP_STRIDE
    d = w2.shape[0]
    lane_step = 0 if slab_step else d
    pe_g = jnp.stack([jnp.pad(pe, ((0, 0), (j * lane_step, LANES - d - j * lane_step))) for j in range(g)])
    w1_g = jnp.stack([jnp.pad(w1.reshape(CMP_LEN, d, d), ((0, 0), (j * lane_step, LANES - d - j * lane_step),
                                                            (0, LANES - d))) for j in range(g)]).astype(BF16)
    w2p = jnp.pad(w2, ((0, LANES - d), (0, LANES - d))).astype(BF16)
    return pl.pallas_call(
        _cmp_kernel,
        grid=(b, g),
        in_specs=[pl.BlockSpec((1, s, LANES), lambda i, j: (i, 0, slab0 + j * slab_step)),
                  pl.BlockSpec((1, CMP_LEN, LANES), lambda i, j: (j, 0, 0)),
                  pl.BlockSpec((1, CMP_LEN, LANES, LANES), lambda i, j: (j, 0, 0, 0)),
                  _const_spec((LANES, LANES)), _const_spec((n, LANES))],
        out_specs=pl.BlockSpec((1, 1, n, LANES), lambda i, j: (i, j, 0, 0)),
        out_shape=jax.ShapeDtypeStruct((b, g, n, LANES), BF16),
        compiler_params=pltpu.CompilerParams(dimension_semantics=("parallel", "parallel")),
        name="compress_%d" % d,
    )(pf3, pe_g, w1_g, w2p, tab)


def _split3(x):
    p1 = x.astype(BF16)
    r1 = x - p1.astype(F32)
    p2 = r1.astype(BF16)
    p3 = (r1 - p2.astype(F32)).astype(BF16)
    return p1, p2, p3


def _nsa_body(q_ref, ng_ref, kc_ref, vc_ref, ks_ref, kw_ref, vs_ref, win_ref, posk_ref, oh_ref, qal_ref,
              slope_ref, ovt_ref, ones_ref, o_ref, m_ref, acc_ref, s_ref, mx_ref, part_ref, cmax_ref, *, topn, fast):
    hpg = NSA_HPG
    tq = Q_TILE
    t0 = pl.program_id(2) * tq
    n_sel = ovt_ref.shape[0]
    n_cmp = kc_ref.shape[2]
    pair_rows = PAIR * tq
    n_pairs = hpg // PAIR
    bounds = [(c * pair_rows, (c + 1) * pair_rows) for c in range(n_pairs)]
    lane_q = lax.broadcasted_iota(jnp.int32, (tq, LANES), 1)

    q_all = q_ref[0]
    q_heads = [q_all[:, h * LANES:(h + 1) * LANES] for h in range(hpg)]
    if fast:
        t_abs = (t0 + lax.broadcasted_iota(jnp.int32, (tq, LANES), 0)).astype(F32)
        cmax = cmax_ref[...]
        spread = [_dot(jnp.abs(q_heads[h]), cmax) for h in range(hpg)]
        m_up = [spread[h] + slope_ref[0, h * tq:(h + 1) * tq, :] * t_abs + 1.0 for h in range(hpg)]
        m1 = [m.astype(BF16) for m in m_up]
        m2 = [(m_up[h] - m1[h].astype(F32)).astype(BF16) for h in range(hpg)]
        q_heads = [jnp.where(lane_q == BOUND_COL, -m1[h], jnp.where(lane_q == BOUND_COL + 1, -m2[h], q_heads[h]))
                   for h in range(hpg)]
    q128 = jnp.concatenate(q_heads, axis=0) + qal_ref[0]

    def softmax_num(s):
        return jnp.exp2(s) if fast else jnp.exp2(s - jnp.max(s, axis=-1, keepdims=True))

    t_c = t0 + lax.broadcasted_iota(jnp.int32, (tq, n_cmp), 0)
    end_c = lax.broadcasted_iota(jnp.int32, (tq, n_cmp), 1) * CMP_STRIDE + (CMP_LEN - 1)
    bias_c = jnp.where(end_c <= t_c, 0.0, NEG)[None]
    seen = (t0 + lax.broadcasted_iota(jnp.int32, (tq, 1), 0)) >= CMP_LEN - 1
    kc = kc_ref[0, 0]
    vc = vc_ref[0, 0] + ones_ref[0:n_cmp, :]
    psum = jnp.zeros((tq, n_cmp), F32)
    l_min = jnp.full((PAIR, tq, 1), 1.0, F32)
    acc_c = []
    for r0, r1 in bounds:
        e_c = softmax_num(_dot_nt(q128[r0:r1], kc).reshape(PAIR, tq, n_cmp) + bias_c)
        acc_c.append(_dot(e_c.reshape(pair_rows, n_cmp).astype(BF16), vc))
        l_c = jnp.sum(e_c, axis=-1, keepdims=True)
        l_min = jnp.minimum(l_min, jnp.where(seen[None], l_c, 1.0))
        psum = psum + jnp.sum(e_c * jnp.where(seen[None], 1.0 / l_c, 0.0), axis=0)

    ovt = ovt_ref[...]
    imp_t = sum(_dot_nt(ovt, piece) for piece in _split3(psum))
    jj = lax.broadcasted_iota(jnp.int32, (n_sel, tq), 0)
    tt = t0 + lax.broadcasted_iota(jnp.int32, (n_sel, tq), 1)
    cur = tt // SEL_LEN
    forced = (jj == 0) | (jj == cur) | (jj == cur - 1)
    score = jnp.where(forced, 1e6, jnp.where(jj * SEL_LEN <= tt, imp_t, -1e6))
    sub = lax.broadcasted_iota(jnp.int32, (SUBLANES, tq), 0)
    groups = [score[SUBLANES * v:SUBLANES * (v + 1), :] for v in range(n_sel // SUBLANES)]
    ranks = [jnp.zeros((SUBLANES, tq), F32) for _ in groups]
    for i in range(n_sel):
        vi, ri = divmod(i, SUBLANES)
        si = jnp.broadcast_to(groups[vi][ri:ri + 1, :], (SUBLANES, tq))
        for v, sv in enumerate(groups):
            if v > vi:
                before = jnp.where(si >= sv, 1.0, 0.0)
            elif v < vi:
                before = jnp.where(si > sv, 1.0, 0.0)
            else:
                before = jnp.where(sub > ri, jnp.where(si >= sv, 1.0, 0.0), jnp.where(si > sv, 1.0, 0.0))
            ranks[v] = ranks[v] + before
    pen_t = jnp.where(jnp.concatenate(ranks, axis=0) < float(topn), 0.0, NEG)
    pen_t = jnp.concatenate([pen_t, jnp.zeros((LANES - n_sel, tq), F32)], axis=0)
    pen = pen_t.T.astype(BF16)
    q256 = jnp.concatenate([q128, jnp.concatenate([pen] * hpg, axis=0)], axis=1)

    gate = jax.nn.sigmoid(ng_ref[0])
    low = lane_q < NSA_DV
    l_acc = [jnp.full((pair_rows, LANES), 1.0, F32)]

    def gated(acc_pair, branch, c):
        if branch:
            l_acc[0] = jnp.minimum(l_acc[0], jnp.where(jnp.concatenate([low] * PAIR, axis=0), 1.0, acc_pair))
        a_e, a_o = acc_pair[:tq], acc_pair[tq:]
        keep = jnp.where(low, a_e, a_o)
        swap = pltpu.roll(jnp.where(low, a_o, a_e), NSA_DV, 1)
        col = branch * hpg + PAIR * c
        g2 = jnp.where(low, gate[:, col:col + 1], gate[:, col + 1:col + 2])
        return g2 * jnp.where(low, keep, swap) / jnp.where(low, swap, keep)

    win = win_ref[0]
    for c in range(n_pairs):
        part_ref[:, c * LANES:(c + 1) * LANES] = (jnp.where(seen, gated(acc_c[c], 0, c), 0.0)
                                                  + win[:, c * LANES:(c + 1) * LANES].astype(F32))

    tk = SEL_KEY_TILE
    ones_v = ones_ref[0:tk, :]
    n_full = t0 // tk

    def keys(kt):
        k0 = pl.multiple_of(kt * tk, tk)
        return jnp.concatenate([ks_ref[0, pl.ds(k0, tk), :] + posk_ref[pl.ds(k0, tk), :],
                                oh_ref[pl.ds(k0, tk), :]], axis=1)

    def logits(kt):
        return _dot_nt(q256, keys(kt))

    def values(kt):
        return vs_ref[0, pl.ds(pl.multiple_of(kt * tk, tk), tk), :] + ones_v

    def row_max(s):
        return jnp.broadcast_to(jnp.max(s, axis=-1, keepdims=True), (s.shape[0], LANES))

    pos_d = n_full * tk + lax.broadcasted_iota(jnp.int32, (tq, tk), 1)
    t_d = t0 + lax.broadcasted_iota(jnp.int32, (tq, tk), 0)
    bias_d = jnp.where(pos_d <= t_d, 0.0, NEG)[None]
    k_d, v_d = keys(n_full), values(n_full)
    for r0, r1 in bounds:
        s_d = (_dot_nt(q256[r0:r1], k_d).reshape(PAIR, tq, tk) + bias_d).reshape(pair_rows, tk)
        if fast:
            p_d = jnp.exp2(s_d)
        else:
            m_d = row_max(s_d)
            m_ref[r0:r1] = m_d
            p_d = jnp.exp2(s_d - jnp.concatenate([m_d] * (tk // LANES), axis=1))
        acc_ref[r0:r1] = _dot(p_d.astype(BF16), v_d)

    if fast:
        s_ref[0] = logits(0)
        last = jnp.maximum(n_full - 1, 0)

        def body(j, carry):
            ka, kb, kc2 = 2 * j, jnp.minimum(2 * j + 1, last), jnp.minimum(2 * j + 2, last)
            s_a = s_ref[0]
            s_ref[1] = logits(kb)
            acc_ref[...] = acc_ref[...] + _dot(jnp.exp2(s_a).astype(BF16), values(ka))
            s_b = s_ref[1]
            s_ref[0] = logits(kc2)
            v_b = values(kb)
            v_b = jnp.where(2 * j + 1 < n_full, v_b, jnp.zeros_like(v_b))
            acc_ref[...] = acc_ref[...] + _dot(jnp.exp2(s_b).astype(BF16), v_b)
            return carry

        n_steps = (n_full + 1) // 2
    else:
        k_0, k_1 = keys(0), keys(jnp.minimum(1, n_full))
        for r0, r1 in bounds:
            s_first = _dot_nt(q256[r0:r1], k_0)
            s_ref[0, r0:r1] = s_first
            mx_ref[0, r0:r1] = row_max(s_first)
            s_ref[1, r0:r1] = _dot_nt(q256[r0:r1], k_1)

        def body(kt, carry):
            s = s_ref[kt % 3]
            m_cur = mx_ref[kt % 2]
            mx_next = row_max(s_ref[(kt + 1) % 3])
            s_next = logits(jnp.minimum(kt + 2, n_full))
            m_prev = m_ref[...]
            m_new = jnp.maximum(m_prev, m_cur)
            p = jnp.exp2(s - jnp.concatenate([m_new] * (tk // LANES), axis=1))
            acc_ref[...] = acc_ref[...] * jnp.exp2(m_prev - m_new) + _dot(p.astype(BF16), values(kt))
            m_ref[...] = m_new
            mx_ref[(kt + 1) % 2] = mx_next
            s_ref[(kt + 2) % 3] = s_next
            return carry

        n_steps = n_full

    lax.fori_loop(0, n_steps, body, 0)
    for c, (r0, r1) in enumerate(bounds):
        sel = gated(acc_ref[r0:r1], 1, c)
        o_ref[0, :, c * LANES:(c + 1) * LANES] = (part_ref[:, c * LANES:(c + 1) * LANES] + sel).astype(o_ref.dtype)
    return jnp.minimum(jnp.min(l_min), jnp.min(l_acc[0]))


def _nsa_kernel(q_ref, ng_ref, kc_ref, vc_ref, ks_ref, kw_ref, vs_ref, win_ref, posk_ref, oh_ref, qal_ref, slope_ref,
                ovt_ref, ones_ref, o_ref, m_ref, acc_ref, s_ref, mx_ref, part_ref, cmax_ref, *, topn):
    refs = (q_ref, ng_ref, kc_ref, vc_ref, ks_ref, kw_ref, vs_ref, win_ref, posk_ref, oh_ref, qal_ref, slope_ref,
            ovt_ref, ones_ref, o_ref, m_ref, acc_ref, s_ref, mx_ref, part_ref, cmax_ref)

    @pl.when(pl.program_id(2) == 0)
    def _():
        def col_max(k):
            return jnp.max(jnp.abs(k.astype(F32)), axis=0, keepdims=True)

        cm = jnp.maximum(jnp.maximum(col_max(ks_ref[0]), col_max(kw_ref[0])), col_max(kc_ref[0, 0]))
        feat = lax.broadcasted_iota(jnp.int32, (1, LANES), 1) < NSA_DK
        cmax_ref[...] = jnp.broadcast_to(jnp.where(feat, cm, 0.0), (LANES, LANES)).T.astype(BF16)

    l_min = _nsa_body(*refs, topn=topn, fast=True)

    @pl.when(jnp.logical_not(l_min > DENOM_FLOOR))
    def _():
        _nsa_body(*refs, topn=topn, fast=False)


def _nsa(pa3, pf3, kcmp, vcmp):
    b, s, _ = pa3.shape
    g, hpg = NSA_GROUPS, NSA_HPG
    n_sel = s // SEL_LEN
    n_cmp = kcmp.shape[2]
    topn = min(SEL_TOPN, n_sel)
    rows = hpg * Q_TILE

    pos = jnp.arange(s, dtype=jnp.int32)
    ab = jnp.stack([pos // LANES] * 3 + [pos % LANES] * 3, axis=1).astype(F32)
    posk = jnp.zeros((s, LANES), F32).at[:, ALIBI_COL:ALIBI_COL + 6].set(ab)
    posk = posk.at[:, BOUND_COL:BOUND_COL + 2].set(1.0).astype(BF16)
    onehot = (pos[:, None] // SEL_LEN == jnp.arange(LANES)[None, :]).astype(BF16)
    slopes = LOG2E * 2.0 ** (-8.0 * (jnp.arange(NSA_HEADS, dtype=F32) + 1.0) / NSA_HEADS)
    s1, s2, s3 = _split3(slopes)
    qcols = jnp.stack([s1.astype(F32) * LANES, s2.astype(F32) * LANES, s3.astype(F32) * LANES,
                       s1.astype(F32), s2.astype(F32), s3.astype(F32)], axis=1)
    qal = jnp.zeros((NSA_HEADS, LANES), F32).at[:, ALIBI_COL:ALIBI_COL + 6].set(qcols)
    qal = jnp.broadcast_to(qal.reshape(g, hpg, 1, LANES), (g, hpg, Q_TILE, LANES)).reshape(g, rows, LANES)
    qal = qal.astype(BF16)
    slope_rows = jnp.broadcast_to(slopes.reshape(g, hpg, 1, 1), (g, hpg, Q_TILE, LANES)).reshape(g, rows, LANES)
    cstart = jnp.arange(n_cmp) * CMP_STRIDE
    sstart = jnp.arange(n_sel) * SEL_LEN
    ovt = ((cstart[None, :] < sstart[:, None] + SEL_LEN) & (cstart[None, :] + CMP_LEN > sstart[:, None])
           & (jnp.arange(n_cmp)[None, :] < n_cmp - 1)).astype(BF16)
    ones = jnp.zeros((WIN_KEYS, LANES), BF16).at[:, ONES_COL:].set(1.0)

    def seq_spec(col0):
        return pl.BlockSpec((1, s, LANES), lambda i, j, t: (i, 0, col0 // LANES + j))

    win = _window(pa3, pf3, posk, qal, ones)
    return pl.pallas_call(
        functools.partial(_nsa_kernel, topn=topn),
        grid=(b, g, s // Q_TILE),
        in_specs=[
            pl.BlockSpec((1, Q_TILE, hpg * LANES), lambda i, j, t: (i, t, PA_NQ // (hpg * LANES) + j)),
            pl.BlockSpec((1, Q_TILE, LANES), lambda i, j, t: (i, t, PF_NG // LANES + j)),
            pl.BlockSpec((1, 1, n_cmp, LANES), lambda i, j, t: (i, j, 0, 0)),
            pl.BlockSpec((1, 1, n_cmp, LANES), lambda i, j, t: (i, j, 0, 0)),
            seq_spec(PA_KS), seq_spec(PA_KW), seq_spec(PA_VS),
            pl.BlockSpec((1, Q_TILE, hpg * NSA_DV), lambda i, j, t: (i, t, j)),
            _const_spec((s, LANES)), _const_spec((s, LANES)),
            pl.BlockSpec((1, rows, LANES), lambda i, j, t: (j, 0, 0)),
            pl.BlockSpec((1, rows, LANES), lambda i, j, t: (j, 0, 0)),
            _const_spec((n_sel, n_cmp)), _const_spec((WIN_KEYS, LANES)),
        ],
        out_specs=pl.BlockSpec((1, Q_TILE, hpg * NSA_DV), lambda i, j, t: (i, t, j)),
        out_shape=jax.ShapeDtypeStruct((b, s, NSA_HEADS * NSA_DV), BF16),
        scratch_shapes=[pltpu.VMEM((rows, LANES), F32), pltpu.VMEM((rows, LANES), F32),
                        pltpu.VMEM((3, rows, SEL_KEY_TILE), F32), pltpu.VMEM((2, rows, LANES), F32),
                        pltpu.VMEM((Q_TILE, hpg * NSA_DV), F32), pltpu.VMEM((LANES, LANES), BF16)],
        compiler_params=pltpu.CompilerParams(dimension_semantics=("parallel", "parallel", "arbitrary"),
                                             vmem_limit_bytes=VMEM_LIMIT),
        name="nsa_attention",
    )(pa3, pf3, kcmp, vcmp, pa3, pa3, pa3, win, posk, onehot, qal, slope_rows, ovt, ones)


def _win_kernel(q_ref, ng_ref, kw_ref, vw_ref, posk_ref, qal_ref, ones_ref, o_ref):
    hpg, tq = NSA_HPG, Q_TILE
    pair_rows = PAIR * tq
    t0 = pl.program_id(2) * tq
    q_all = q_ref[0]
    q128 = jnp.concatenate([q_all[:, h * LANES:(h + 1) * LANES] for h in range(hpg)], axis=0) + qal_ref[0]
    gate = jax.nn.sigmoid(ng_ref[0])
    low = lax.broadcasted_iota(jnp.int32, (tq, LANES), 1) < NSA_DV
    w0 = pl.multiple_of(jnp.maximum(t0 - WINDOW, 0), LANES)
    kw = kw_ref[0, pl.ds(w0, WIN_KEYS), :] + posk_ref[pl.ds(w0, WIN_KEYS), :]
    vw = vw_ref[0, pl.ds(w0, WIN_KEYS), :] + ones_ref[...]
    pos_w = w0 + lax.broadcasted_iota(jnp.int32, (tq, WIN_KEYS), 1)
    t_w = t0 + lax.broadcasted_iota(jnp.int32, (tq, WIN_KEYS), 0)
    bias_w = jnp.where((pos_w <= t_w) & (pos_w > t_w - WINDOW), 0.0, NEG)[None]
    n_pairs = hpg // PAIR
    s_w = [_dot_nt(q128[c * pair_rows:(c + 1) * pair_rows], kw).reshape(PAIR, tq, WIN_KEYS) + bias_w
           for c in range(n_pairs)]
    e_w = [jnp.exp2(s - jnp.max(s, axis=-1, keepdims=True)) for s in s_w]
    accs = [_dot(e.reshape(pair_rows, WIN_KEYS).astype(BF16), vw) for e in e_w]
    for c in range(n_pairs):
        a_e, a_o = accs[c][:tq], accs[c][tq:]
        keep = jnp.where(low, a_e, a_o)
        swap = pltpu.roll(jnp.where(low, a_o, a_e), NSA_DV, 1)
        col = 2 * hpg + PAIR * c
        g2 = jnp.where(low, gate[:, col:col + 1], gate[:, col + 1:col + 2])
        o_ref[0, :, c * LANES:(c + 1) * LANES] = (g2 * jnp.where(low, keep, swap)
                                                  / jnp.where(low, swap, keep)).astype(o_ref.dtype)


def _window(pa3, pf3, posk, qal, ones):
    b, s, _ = pa3.shape
    g, hpg = NSA_GROUPS, NSA_HPG
    rows = hpg * Q_TILE

    def seq_spec(col0):
        return pl.BlockSpec((1, s, LANES), lambda i, j, t: (i, 0, col0 // LANES + j))

    return pl.pallas_call(
        _win_kernel,
        grid=(b, g, s // Q_TILE),
        in_specs=[
            pl.BlockSpec((1, Q_TILE, hpg * LANES), lambda i, j, t: (i, t, PA_NQ // (hpg * LANES) + j)),
            pl.BlockSpec((1, Q_TILE, LANES), lambda i, j, t: (i, t, PF_NG // LANES + j)),
            seq_spec(PA_KW), seq_spec(PA_VW), _const_spec((s, LANES)),
            pl.BlockSpec((1, rows, LANES), lambda i, j, t: (j, 0, 0)), _const_spec((WIN_KEYS, LANES)),
        ],
        out_specs=pl.BlockSpec((1, Q_TILE, hpg * NSA_DV), lambda i, j, t: (i, t, j)),
        out_shape=jax.ShapeDtypeStruct((b, s, NSA_HEADS * NSA_DV), BF16),
        compiler_params=pltpu.CompilerParams(dimension_semantics=("parallel", "parallel", "parallel"),
                                             vmem_limit_bytes=VMEM_LIMIT),
        name="window_attention",
    )(pa3, pf3, pa3, pa3, posk, qal, ones)


def _post_kernel(x_ref, ret_ref, nsa_ref, p_ref, gpre_ref, gpost_ref, gfpre_ref, gfpost_ref, gn_ref,
                 wg_ref, wo_ref, wfi_ref, wfo_ref, wpg_ref, wpl_ref, o_ref, *, ff_chunk):
    d = D_MODEL
    x = x_ref[...]
    h = (_rms(x) * gpre_ref[...]).astype(BF16)
    ret = jax.nn.silu(_dot(h, wg_ref[:, 0:d])) * (ret_ref[...].astype(F32) * gn_ref[...])
    mixin = jax.nn.sigmoid(_dot(h, wg_ref[:, d:2 * d])) * ret
    mixin = mixin + jax.nn.sigmoid(_dot(h, wg_ref[:, 2 * d:3 * d])) * nsa_ref[...].astype(F32)
    x1 = x + _rms(_dot(mixin.astype(BF16), wo_ref[...])) * gpost_ref[...]

    h2 = (_rms(x1) * gfpre_ref[...]).astype(BF16)
    f = jnp.zeros(x.shape, F32)
    for c0 in range(0, D_FF, ff_chunk):
        u = _dot(h2, wfi_ref[:, c0:c0 + ff_chunk])
        v = _dot(h2, wfi_ref[:, D_FF + c0:D_FF + c0 + ff_chunk])
        f = f + _dot((jax.nn.silu(u) * v).astype(BF16), wfo_ref[c0:c0 + ff_chunk, :])
    x2 = x1 + _rms(f) * gfpost_ref[...]

    gate = jax.nn.sigmoid(_dot(_rms(x2).astype(BF16), wpg_ref[...]))
    o_ref[...] = x2 + gate * _dot(p_ref[...].astype(BF16), wpl_ref[...])


def _post(x2, ret2, nsa2, p2, gpre, gpost, gfpre, gfpost, gn, wg, wo, wfi, wfo, wpg, wpl, tm, ff_chunk):
    t, d = x2.shape

    def row_spec(w):
        return pl.BlockSpec((tm, w), lambda i: (i, 0))

    vec = _const_spec((1, d))
    return pl.pallas_call(
        functools.partial(_post_kernel, ff_chunk=ff_chunk),
        grid=(t // tm,),
        in_specs=[row_spec(d), row_spec(d), row_spec(d), row_spec(PLE_DIM), vec, vec, vec, vec, vec,
                  _const_spec(wg.shape), _const_spec(wo.shape), _const_spec(wfi.shape), _const_spec(wfo.shape),
                  _const_spec(wpg.shape), _const_spec(wpl.shape)],
        out_specs=row_spec(d),
        out_shape=jax.ShapeDtypeStruct((t, d), F32),
        compiler_params=pltpu.CompilerParams(dimension_semantics=("parallel",), vmem_limit_bytes=VMEM_LIMIT),
        name="post",
    )(x2, ret2, nsa2, p2, gpre, gpost, gfpre, gfpost, gn, wg, wo, wfi, wfo, wpg, wpl)


def _pad_slabs(w, n, width):
    d = w.shape[0]
    return jnp.pad(w.reshape(d, n, width), ((0, 0), (0, 0), (0, LANES - width))).reshape(d, n * LANES)


def _layer(x, p, g_mix_pre, g_mix_post, g_ffn_pre, g_ffn_post, w_in, gn_g, pe_k, w1_k, w2_k, pe_v, w1_v, w2_v,
           w_out, w_ffn_in, w_ffn_out, w_ple, w_ple_gate):
    b, s, d = x.shape
    t = b * s
    g, hpg = NSA_GROUPS, NSA_HPG
    offs = [0]
    for n in IN_SPLITS:
        offs.append(offs[-1] + n)
    (w_rq, w_rk, w_rv, w_rg, w_nq, w_kc, w_vc, w_ks, w_vs, w_kw, w_vw, w_ng, w_mg) = [
        w_in[:, offs[i]:offs[i + 1]] for i in range(len(IN_SPLITS))]

    w_a = jnp.concatenate([
        w_rq, w_rk * (RET_DK ** -0.5), w_rv,
        _pad_slabs(w_nq * (LOG2E * NSA_DK ** -0.5), NSA_HEADS, NSA_DK),
        _pad_slabs(w_ks, g, NSA_DK), _pad_slabs(w_kw, g, NSA_DK),
        _pad_slabs(w_vs, g, NSA_DV), _pad_slabs(w_vw, g, NSA_DV)], axis=1).astype(BF16)
    w_ng_g = w_ng.reshape(d, 3, g, hpg).transpose(0, 2, 1, 3).reshape(d, g, 3 * hpg)
    w_f = jnp.concatenate([
        _pad_slabs(w_kc, g, NSA_DK), w_vc,
        jnp.pad(w_ng_g, ((0, 0), (0, 0), (0, LANES - 3 * hpg))).reshape(d, g * LANES)], axis=1).astype(BF16)

    x2 = x.reshape(t, d)
    gpre = g_mix_pre.reshape(1, d)
    pa, pf = _proj(x2, gpre, w_a, w_f, 512, 1280)
    pa3 = pa.reshape(b, s, PA_TOTAL)
    pf3 = pf.reshape(b, s, PF_TOTAL)

    ret = _retention(pa3)

    nh = s // CMP_STRIDE
    cend = jnp.arange(nh, dtype=jnp.int32) * CMP_STRIDE + (CMP_LEN - 1)
    ab = jnp.stack([cend // LANES] * 3 + [cend % LANES] * 3, axis=1).astype(F32)
    posc = jnp.zeros((nh, LANES), F32).at[:, ALIBI_COL:ALIBI_COL + 6].set(ab).at[:, BOUND_COL:BOUND_COL + 2].set(1.0)
    kcmp = _compress(pf3, PF_KC // LANES, 1, pe_k, w1_k, w2_k, posc)
    vcmp = _compress(pf3, PF_VC // LANES, 0, pe_v, w1_v, w2_v, jnp.zeros((nh, LANES), F32))

    nsa = _nsa(pa3, pf3, kcmp, vcmp)

    w_g = jnp.concatenate([w_rg, w_mg], axis=1).astype(BF16)
    out = _post(x2, ret.reshape(t, d), nsa.reshape(t, d), p.reshape(t, PLE_DIM),
                gpre, g_mix_post.reshape(1, d), g_ffn_pre.reshape(1, d), g_ffn_post.reshape(1, d),
                gn_g.reshape(1, d), w_g, w_out.astype(BF16), w_ffn_in.astype(BF16), w_ffn_out.astype(BF16),
                w_ple_gate.astype(BF16), w_ple.astype(BF16), 512, 256)
    return out.reshape(b, s, d)


def kernel(x, p, norm_mix_pre, norm_mix_post, norm_ffn_pre, norm_ffn_post, w_in, ret_gn_g, cmp_pe_k, cmp_w1_k,
           cmp_w2_k, cmp_pe_v, cmp_w1_v, cmp_w2_v, w_out, w_ffn_in, w_ffn_out, w_ple, w_ple_gate):
    for i in range(w_in.shape[0]):
        x = _layer(x, p[i], norm_mix_pre[i], norm_mix_post[i], norm_ffn_pre[i], norm_ffn_post[i], w_in[i],
                   ret_gn_g[i], cmp_pe_k[i], cmp_w1_k[i], cmp_w2_k[i], cmp_pe_v[i], cmp_w1_v[i], cmp_w2_v[i],
                   w_out[i], w_ffn_in[i], w_ffn_out[i], w_ple[i], w_ple_gate[i])
    return x
```
